```python
import jax, jax.numpy as jnp
from jax import lax
import numpy as np

D_MODEL = 2048
BATCH = 4
SEQ = 2048
DEPTH = 1

N_HEADS = 16
HEAD_DIM = D_MODEL // N_HEADS
ATTN_WIDTH = N_HEADS * HEAD_DIM
CONV_WIDTH = D_MODEL
CONV_KERNEL = 3
MOBA_BLOCK = 256
MOBA_TOPK = 3
Q_CHUNK = 16
D_FF = ((8 * D_MODEL // 3 + 255) // 256) * 256
RMS_EPS = 1e-6
IN_WIDTH = 3 * ATTN_WIDTH + 3 * CONV_WIDTH + 2 * D_MODEL

kernel_name = 'macaron_hybrid_moba_shortconv_alibi'


def rms_norm(x, g):
    xf = x.astype(jnp.float32)
    y = xf * lax.rsqrt(jnp.mean(xf * xf, axis=-1, keepdims=True) + RMS_EPS)
    return (y * g.astype(jnp.float32)).astype(x.dtype)


def swiglu(h, w_gate, w_up, w_down):
    return (jax.nn.silu(h @ w_gate) * (h @ w_up)) @ w_down


def alibi_slopes():
    return jnp.exp2(-8.0 * jnp.arange(1, N_HEADS + 1, dtype=jnp.float32) / N_HEADS)


def moba_attention(q, k, v):
    bsz, s, h, hd = q.shape
    nb = -(-s // MOBA_BLOCK)
    s_pad = nb * MOBA_BLOCK
    pad = ((0, 0), (0, s_pad - s), (0, 0), (0, 0))
    q, k, v = [jnp.pad(a, pad).transpose(0, 2, 1, 3) for a in (q, k, v)]
    kb = k.reshape(bsz, h, nb, MOBA_BLOCK, hd)
    vb = v.reshape(bsz, h, nb, MOBA_BLOCK, hd)
    slopes = alibi_slopes()
    scale = HEAD_DIM ** -0.5

    kmean = jnp.mean(kb.astype(jnp.float32), axis=3)
    gate = jnp.einsum('bhqd,bhnd->bhqn', q.astype(jnp.float32), kmean)
    qblk = jnp.arange(s_pad) // MOBA_BLOCK
    past = jnp.arange(nb)[None, :] < qblk[:, None]
    gate = jnp.where(past, gate, -jnp.inf)
    n_sel = min(MOBA_TOPK, nb)
    _, sel = lax.top_k(gate, n_sel)

    n_chunks = s_pad // Q_CHUNK
    q_chunks = jnp.moveaxis(q.reshape(bsz, h, n_chunks, Q_CHUNK, hd), 2, 0)
    sel_chunks = jnp.moveaxis(sel.reshape(bsz, h, n_chunks, Q_CHUNK, n_sel), 2, 0)
    starts = jnp.arange(n_chunks, dtype=jnp.int32) * Q_CHUNK
    bi = jnp.arange(bsz)[:, None, None, None]
    hi = jnp.arange(h)[None, :, None, None]
    offs = jnp.arange(MOBA_BLOCK)

    def one_chunk(args):
        qc, selc, start = args
        tq = start + jnp.arange(Q_CHUNK)
        own = start // MOBA_BLOCK
        kg = kb[bi, hi, selc]
        vg = vb[bi, hi, selc]
        ko = lax.dynamic_index_in_dim(kb, own, axis=2, keepdims=False)
        vo = lax.dynamic_index_in_dim(vb, own, axis=2, keepdims=False)
        kpos_sel = selc[..., None] * MOBA_BLOCK + offs
        dist_sel = (tq[:, None, None] - kpos_sel).astype(jnp.float32)
        s_sel = jnp.einsum('bhqd,bhqnjd->bhqnj', qc, kg,
                           preferred_element_type=jnp.float32) * scale
        s_sel = s_sel - slopes[:, None, None, None] * dist_sel
        s_sel = jnp.where((selc < own)[..., None], s_sel, -jnp.inf)
        kpos_own = own * MOBA_BLOCK + offs
        dist_own = (tq[:, None] - kpos_own[None, :]).astype(jnp.float32)
        s_own = jnp.einsum('bhqd,bhjd->bhqj', qc, ko,
                           preferred_element_type=jnp.float32) * scale
        s_own = s_own - slopes[:, None, None] * dist_own
        s_own = jnp.where(dist_own >= 0, s_own, -jnp.inf)
        scores = jnp.concatenate(
            [s_sel.reshape(bsz, h, Q_CHUNK, n_sel * MOBA_BLOCK), s_own], axis=-1)
        p = jax.nn.softmax(scores, axis=-1).astype(vg.dtype)
        p_sel = p[..., :n_sel * MOBA_BLOCK].reshape(bsz, h, Q_CHUNK, n_sel, MOBA_BLOCK)
        p_own = p[..., n_sel * MOBA_BLOCK:]
        return (jnp.einsum('bhqnj,bhqnjd->bhqd', p_sel, vg)
                + jnp.einsum('bhqj,bhjd->bhqd', p_own, vo))

    out = lax.map(one_chunk, (q_chunks, sel_chunks, starts))
    out = jnp.moveaxis(out, 0, 2).reshape(bsz, h, s_pad, hd)
    out = out.transpose(0, 2, 1, 3)[:, :s]
    return out.reshape(bsz, s, h * hd)


def short_conv(u, conv_w):
    rhs = conv_w[:, None, :]
    return lax.conv_general_dilated(
        u, rhs, window_strides=(1,), padding=[(CONV_KERNEL - 1, 0)],
        dimension_numbers=('NWC', 'WIO', 'NWC'), feature_group_count=u.shape[-1])


def hybrid_mixer(h, w_in, b_gate, conv_w, w_attn_out, w_conv_out, w_out):
    bsz, s, _ = h.shape
    z = h @ w_in
    cuts = np.cumsum([ATTN_WIDTH] * 3 + [CONV_WIDTH] * 3).tolist()
    q, k, v, gb, gc, xt, g_logits = jnp.split(z, cuts, axis=-1)
    g_logits = (g_logits + b_gate).astype(jnp.float32)
    g_attn = jax.nn.sigmoid(g_logits[..., :D_MODEL]).astype(h.dtype)
    g_conv = jax.nn.sigmoid(g_logits[..., D_MODEL:]).astype(h.dtype)
    shp = (bsz, s, N_HEADS, HEAD_DIM)
    y_attn = moba_attention(q.reshape(shp), k.reshape(shp), v.reshape(shp)) @ w_attn_out
    y_conv = (gb * short_conv(gc * xt, conv_w)) @ w_conv_out
    return (g_attn * y_attn + g_conv * y_conv) @ w_out


def setup_inputs(seed: int = 0) -> dict:
    key = jax.random.key(seed)
    ks = jax.random.split(key, 20)

    def nrm(k, shape, fan_in):
        return jax.random.normal(k, shape, jnp.float32) * (fan_in ** -0.5)

    def gain(k):
        return 1.0 + 0.01 * jax.random.normal(k, (DEPTH, D_MODEL), jnp.float32)

    return {
        'x': jax.random.normal(ks[0], (BATCH, SEQ, D_MODEL), jnp.float32),
        'ffn1_norm': gain(ks[1]),
        'ffn1_w_gate': nrm(ks[2], (DEPTH, D_MODEL, D_FF), D_MODEL),
        'ffn1_w_up': nrm(ks[3], (DEPTH, D_MODEL, D_FF), D_MODEL),
        'ffn1_w_down': nrm(ks[4], (DEPTH, D_FF, D_MODEL), D_FF),
        'mix_norm': gain(ks[5]),
        'w_in': nrm(ks[6], (DEPTH, D_MODEL, IN_WIDTH), D_MODEL),
        'b_gate': 0.01 * jax.random.normal(ks[7], (DEPTH, 2 * D_MODEL), jnp.float32),
        'conv_w': nrm(ks[8], (DEPTH, CONV_KERNEL, CONV_WIDTH), CONV_KERNEL),
        'w_attn_out': nrm(ks[9], (DEPTH, ATTN_WIDTH, D_MODEL), ATTN_WIDTH),
        'w_conv_out': nrm(ks[10], (DEPTH, CONV_WIDTH, D_MODEL), CONV_WIDTH),
        'w_out': nrm(ks[11], (DEPTH, D_MODEL, D_MODEL), D_MODEL),
        'ffn2_norm': gain(ks[12]),
        'ffn2_w_gate': nrm(ks[13], (DEPTH, D_MODEL, D_FF), D_MODEL),
        'ffn2_w_up': nrm(ks[14], (DEPTH, D_MODEL, D_FF), D_MODEL),
        'ffn2_w_down': nrm(ks[15], (DEPTH, D_FF, D_MODEL), D_FF),
        'final_norm': 1.0 + 0.01 * jax.random.normal(ks[16], (D_MODEL,), jnp.float32),
    }


def reference(x, ffn1_norm, ffn1_w_gate, ffn1_w_up, ffn1_w_down, mix_norm, w_in, b_gate,
              conv_w, w_attn_out, w_conv_out, w_out, ffn2_norm, ffn2_w_gate, ffn2_w_up,
              ffn2_w_down, final_norm):
    for l in range(DEPTH):
        x = x + 0.5 * swiglu(rms_norm(x, ffn1_norm[l]), ffn1_w_gate[l], ffn1_w_up[l], ffn1_w_down[l])
        x = x + hybrid_mixer(rms_norm(x, mix_norm[l]), w_in[l], b_gate[l], conv_w[l],
                             w_attn_out[l], w_conv_out[l], w_out[l])
        x = x + 0.5 * swiglu(rms_norm(x, ffn2_norm[l]), ffn2_w_gate[l], ffn2_w_up[l], ffn2_w_down[l])
    return rms_norm(x, final_norm)
```

```python
import functools

import jax
import jax.numpy as jnp
from jax import lax
from jax.experimental import pallas as pl
from jax.experimental.pallas import tpu as pltpu

F32 = jnp.float32
BF16 = jnp.bfloat16

N_HEADS = 16
HEAD_DIM = 128
CONV_KERNEL = 3
MOBA_BLOCK = 256
MOBA_TOPK = 3
RMS_EPS = 1e-6
N_IN_SEGMENTS = 8

V7X_VMEM_BYTES = 64 * 1024 * 1024
VMEM_LIMIT_BYTES = V7X_VMEM_BYTES - 8 * 1024 * 1024

MASK_NEG = -1e30


def _params(*semantics):
    return pltpu.CompilerParams(dimension_semantics=semantics, vmem_limit_bytes=VMEM_LIMIT_BYTES)


def _rms_norm(x, g):
    return x * lax.rsqrt(jnp.mean(x * x, axis=-1, keepdims=True) + RMS_EPS) * g


def _dot(a, b):
    return jnp.dot(a, b, preferred_element_type=F32)


def _dot_nt(a, b):
    return lax.dot_general(a, b, (((1,), (1,)), ((), ())), preferred_element_type=F32)


def _ffn_kernel(x_ref, gin_ref, wg_ref, wu_ref, wd_ref, gout_ref, *rest, emit_residual):
    if emit_residual:
        y_ref, n_ref, h_ref, acc_ref = rest
    else:
        n_ref, h_ref, acc_ref = rest
    f = pl.program_id(1)

    @pl.when(f == 0)
    def _():
        h_ref[...] = _rms_norm(x_ref[...], gin_ref[...]).astype(BF16)

    h = h_ref[...]
    g = _dot(h, wg_ref[...])
    u = _dot(h, wu_ref[...])
    a = (g * jax.nn.sigmoid(g) * u).astype(BF16)
    part = _dot(a, wd_ref[...])

    @pl.when(f == 0)
    def _():
        acc_ref[...] = part

    @pl.when(f > 0)
    def _():
        acc_ref[...] += part

    @pl.when(f == pl.num_programs(1) - 1)
    def _():
        y = x_ref[...] + 0.5 * acc_ref[...]
        if emit_residual:
            y_ref[...] = y
        n_ref[...] = _rms_norm(y, gout_ref[...]).astype(n_ref.dtype)


def _ffn(x, g_in, wg, wu, wd, g_out, *, emit_residual, tm, tf, name):
    t, d = x.shape
    d_ff = wg.shape[1]
    row = pl.BlockSpec((tm, d), lambda i, f: (i, 0))
    vec = pl.BlockSpec((1, d), lambda i, f: (0, 0))
    if emit_residual:
        out_shape = (jax.ShapeDtypeStruct((t, d), F32), jax.ShapeDtypeStruct((t, d), BF16))
        out_specs = (row, row)
    else:
        out_shape = jax.ShapeDtypeStruct((t, d), F32)
        out_specs = row
    return pl.pallas_call(
        functools.partial(_ffn_kernel, emit_residual=emit_residual),
        out_shape=out_shape,
        grid=(t // tm, d_ff // tf),
        in_specs=[
            row, vec,
            pl.BlockSpec((d, tf), lambda i, f: (0, f)),
            pl.BlockSpec((d, tf), lambda i, f: (0, f)),
            pl.BlockSpec((tf, d), lambda i, f: (f, 0)),
            vec,
        ],
        out_specs=out_specs,
        scratch_shapes=[pltpu.VMEM((tm, d), BF16), pltpu.VMEM((tm, d), F32)],
        compiler_params=_params("parallel", "arbitrary"),
        name=name,
    )(x, g_in.reshape(1, d), wg, wu, wd, g_out.reshape(1, d))


def _in_proj_kernel(h_ref, wq_ref, wk_ref, wv_ref, wb_ref, wc_ref, wx_ref, wga_ref, wgc_ref,
                    bga_ref, bgc_ref, cw_ref,
                    q_ref, k_ref, v_ref, cp_ref, ga_ref, gc_ref, u_ref, *, row_chunk):
    seq, tn = q_ref.shape
    halo = 8
    u_ref[0:halo, :] = jnp.zeros((halo, tn), F32)
    cw = cw_ref[...]
    for r0 in range(0, seq, row_chunk):
        rows = pl.ds(r0, row_chunk)
        h = h_ref[rows, :]
        q_ref[rows, :] = _dot(h, wq_ref[...]).astype(BF16)
        k_ref[rows, :] = _dot(h, wk_ref[...]).astype(BF16)
        v_ref[rows, :] = _dot(h, wv_ref[...]).astype(BF16)
        ga_ref[rows, :] = jax.nn.sigmoid(_dot(h, wga_ref[...]) + bga_ref[...]).astype(BF16)
        gc_ref[rows, :] = jax.nn.sigmoid(_dot(h, wgc_ref[...]) + bgc_ref[...]).astype(BF16)
        u = _dot(h, wc_ref[...]) * _dot(h, wx_ref[...])
        u_ref[pl.ds(halo + r0, row_chunk), :] = u
        conv = cw[2:3, :] * u
        for back in range(1, CONV_KERNEL):
            conv += cw[2 - back:3 - back, :] * u_ref[pl.ds(halo + r0 - back, row_chunk), :]
        cp_ref[rows, :] = (_dot(h, wb_ref[...]) * conv).astype(BF16)


def _in_proj(hm, w_in, b_gate, conv_w, *, bsz, seq, tn, row_chunk):
    t, d = hm.shape
    nj = d // tn
    assert w_in.shape == (d, N_IN_SEGMENTS * d) and t == bsz * seq

    def w_spec(seg):
        return pl.BlockSpec((d, tn), lambda b, j, seg=seg: (0, seg * nj + j))

    def b_spec(seg):
        return pl.BlockSpec((1, tn), lambda b, j, seg=seg: (0, seg * nj + j))

    tile = pl.BlockSpec((seq, tn), lambda b, j: (b, j))
    out = jax.ShapeDtypeStruct((t, d), BF16)
    return pl.pallas_call(
        functools.partial(_in_proj_kernel, row_chunk=row_chunk),
        out_shape=(out,) * 6,
        grid=(bsz, nj),
        in_specs=[pl.BlockSpec((seq, d), lambda b, j: (b, 0))]
        + [w_spec(s) for s in range(N_IN_SEGMENTS)]
        + [b_spec(0), b_spec(1), pl.BlockSpec((CONV_KERNEL, tn), lambda b, j: (0, j))],
        out_specs=(tile,) * 6,
        scratch_shapes=[pltpu.VMEM((seq + 8, tn), F32)],
        compiler_params=_params("parallel", "arbitrary"),
        name="in_proj",
    )(hm, *([w_in] * N_IN_SEGMENTS), b_gate.reshape(1, 2 * d), b_gate.reshape(1, 2 * d), conv_w)


def _moba_kernel(slopes_ref, q_ref, k_ref, v_ref, o_ref, qa_ref, ka_ref, va_ref, bias_ref):
    seq, hd = q_ref.shape
    blk = MOBA_BLOCK
    nb = seq // blk
    slope = slopes_ref[pl.program_id(1)]
    scale = hd ** -0.5

    q = q_ref[...]
    k = k_ref[...]

    kmean = jnp.mean(k.astype(F32).reshape(nb, blk, hd), axis=1)
    kmean_hi = kmean.astype(BF16)
    kmean_lo = (kmean - kmean_hi.astype(F32)).astype(BF16)
    gate2 = _dot_nt(jnp.concatenate([kmean_hi, kmean_lo], axis=0), q)
    gate = gate2[:nb] + gate2[nb:]
    blk_id = lax.broadcasted_iota(jnp.int32, (nb, seq), 0)
    q_blk = lax.broadcasted_iota(jnp.int32, (nb, seq), 1) // blk
    gate = jnp.where(blk_id < q_blk, gate, -jnp.inf)
    mask_rows = []
    for n in range(nb):
        g_n = gate[n:n + 1, :]
        beats = jnp.where(gate > g_n, 1.0, jnp.where((gate == g_n) & (blk_id < n), 1.0, 0.0))
        rank = jnp.sum(beats, axis=0, keepdims=True)
        q_blk_row = q_blk[0:1, :]
        visible = ((rank < float(MOBA_TOPK)) & (n < q_blk_row)) | (n == q_blk_row)
        mask_rows.append(jnp.where(visible, 0.0, MASK_NEG))
    mask_t = jnp.concatenate(mask_rows + [jnp.zeros((hd - nb, seq), F32)], axis=0)
    mask_cols = mask_t.T.astype(BF16)

    qa_ref[:, :hd] = q
    qa_ref[:, hd:] = mask_cols
    ka_ref[:, :hd] = k
    lane = lax.broadcasted_iota(jnp.int32, (seq, hd), 1)
    key_blk = lax.broadcasted_iota(jnp.int32, (seq, hd), 0) // blk
    ka_ref[:, hd:] = jnp.where(lane == key_blk, 1.0, 0.0).astype(BF16)
    va_ref[:, :hd] = v_ref[...]
    va_ref[:, hd:] = jnp.ones((seq, hd), BF16)

    diff = (lax.broadcasted_iota(jnp.int32, (blk, blk), 0)
            - lax.broadcasted_iota(jnp.int32, (blk, blk), 1)).astype(F32)
    bias_ref[0] = jnp.where(diff >= 0, slope * diff, jnp.inf)
    for d in range(1, nb):
        bias_ref[d] = slope * (diff + float(d * blk))

    for qb in range(nb):
        qa = qa_ref[qb * blk:(qb + 1) * blk, :]
        scores = []
        for n in range(qb + 1):
            s = _dot_nt(qa, ka_ref[n * blk:(n + 1) * blk, :])
            scores.append(s * scale - bias_ref[qb - n])
        m = scores[0]
        for s in scores[1:]:
            m = jnp.maximum(m, s)
        m = jnp.max(m, axis=-1, keepdims=True)
        acc = None
        for n in range(qb + 1):
            p = jnp.exp(scores[n] - m).astype(BF16)
            pv = _dot(p, va_ref[n * blk:(n + 1) * blk, :])
            acc = pv if acc is None else acc + pv
        o_ref[qb * blk:(qb + 1) * blk, :] = (acc[:, :hd] / acc[:, hd:]).astype(o_ref.dtype)


def _moba(q, k, v, *, bsz, seq):
    t, d = q.shape
    slopes = jnp.exp2(-8.0 * jnp.arange(1, N_HEADS + 1, dtype=F32) / N_HEADS)
    tile = pl.BlockSpec((seq, HEAD_DIM), lambda b, h, slopes: (b, h))
    return pl.pallas_call(
        _moba_kernel,
        out_shape=jax.ShapeDtypeStruct((t, d), BF16),
        grid_spec=pltpu.PrefetchScalarGridSpec(
            num_scalar_prefetch=1,
            grid=(bsz, N_HEADS),
            in_specs=[tile, tile, tile],
            out_specs=tile,
            scratch_shapes=[
                pltpu.VMEM((seq, 2 * HEAD_DIM), BF16),
                pltpu.VMEM((seq, 2 * HEAD_DIM), BF16),
                pltpu.VMEM((seq, 2 * HEAD_DIM), BF16),
                pltpu.VMEM((seq // MOBA_BLOCK, MOBA_BLOCK, MOBA_BLOCK), F32),
            ],
        ),
        compiler_params=_params("parallel", "arbitrary"),
        name="moba",
    )(slopes, q, k, v)


def _mix_kernel(a_ref, c_ref, wa_ref, wc_ref, ga_ref, gc_ref, m_ref):
    ya = _dot(a_ref[...], wa_ref[...])
    yc = _dot(c_ref[...], wc_ref[...])
    m_ref[...] = (ga_ref[...].astype(F32) * ya + gc_ref[...].astype(F32) * yc).astype(m_ref.dtype)


def _mix(attn, conv, wa, wc, ga, gc, *, tm, tn):
    t, d = attn.shape
    row = pl.BlockSpec((tm, d), lambda i, j: (i, 0))
    col = pl.BlockSpec((d, tn), lambda i, j: (0, j))
    tile = pl.BlockSpec((tm, tn), lambda i, j: (i, j))
    return pl.pallas_call(
        _mix_kernel,
        out_shape=jax.ShapeDtypeStruct((t, d), BF16),
        grid=(t // tm, d // tn),
        in_specs=[row, row, col, col, tile, tile],
        out_specs=tile,
        compiler_params=_params("parallel", "arbitrary"),
        name="mix",
    )(attn, conv, wa, wc, ga, gc)


def _out_proj_kernel(m_ref, w_ref, x_ref, o_ref):
    o_ref[...] = x_ref[...] + _dot(m_ref[...], w_ref[...])


def _out_proj(m, w, x, *, tm, tn):
    t, d = m.shape
    tile = pl.BlockSpec((tm, tn), lambda i, j: (i, j))
    return pl.pallas_call(
        _out_proj_kernel,
        out_shape=jax.ShapeDtypeStruct((t, d), F32),
        grid=(t // tm, d // tn),
        in_specs=[pl.BlockSpec((tm, d), lambda i, j: (i, 0)),
                  pl.BlockSpec((d, tn), lambda i, j: (0, j)), tile],
        out_specs=tile,
        compiler_params=_params("parallel", "arbitrary"),
        name="out_proj",
    )(m, w, x)


def kernel(x, ffn1_norm, ffn1_w_gate, ffn1_w_up, ffn1_w_down, mix_norm, w_in, b_gate, conv_w,
           w_attn_out, w_conv_out, w_out, ffn2_norm, ffn2_w_gate, ffn2_w_up, ffn2_w_down, final_norm):
    bsz, seq, d = x.shape
    depth = ffn1_norm.shape[0]
    assert d == N_HEADS * HEAD_DIM and seq % MOBA_BLOCK == 0 and seq // MOBA_BLOCK <= HEAD_DIM
    xs = x.reshape(bsz * seq, d)
    for l in range(depth):
        last = l == depth - 1
        x1, hm = _ffn(xs, ffn1_norm[l], ffn1_w_gate[l].astype(BF16), ffn1_w_up[l].astype(BF16),
                      ffn1_w_down[l].astype(BF16), mix_norm[l],
                      emit_residual=True, tm=512, tf=512, name="ffn1")
        q, k, v, conv, ga, gc = _in_proj(hm, w_in[l].astype(BF16), b_gate[l], conv_w[l],
                                         bsz=bsz, seq=seq, tn=256, row_chunk=512)
        attn = _moba(q, k, v, bsz=bsz, seq=seq)
        m = _mix(attn, conv, w_attn_out[l].astype(BF16), w_conv_out[l].astype(BF16), ga, gc,
                 tm=1024, tn=512)
        x2 = _out_proj(m, w_out[l].astype(BF16), x1, tm=1024, tn=512)
        if last:
            xs = _ffn(x2, ffn2_norm[l], ffn2_w_gate[l].astype(BF16), ffn2_w_up[l].astype(BF16),
                      ffn2_w_down[l].astype(BF16), final_norm,
                      emit_residual=False, tm=512, tf=512, name="ffn2")
        else:
            one = jnp.ones((d,), F32)
            xs, _ = _ffn(x2, ffn2_norm[l], ffn2_w_gate[l].astype(BF16), ffn2_w_up[l].astype(BF16),
                         ffn2_w_down[l].astype(BF16), one,
                         emit_residual=True, tm=512, tf=512, name="ffn2")
    if depth == 0:
        raise NotImplementedError("depth 0 has no Pallas work")
    return xs.reshape(bsz, seq, d)
```

```python
import functools

import jax
import jax.numpy as jnp
from jax import lax
from jax.experimental import pallas as pl
from jax.experimental.pallas import tpu as pltpu

F32 = jnp.float32
BF16 = jnp.bfloat16

N_HEADS = 16
HEAD_DIM = 128
CONV_KERNEL = 3
MOBA_BLOCK = 256
MOBA_TOPK = 3
RMS_EPS = 1e-6
N_IN_SEGMENTS = 8

V7X_VMEM_BYTES = 64 * 1024 * 1024
VMEM_LIMIT_BYTES = V7X_VMEM_BYTES - 8 * 1024 * 1024

MASK_NEG = -1e30


def _params(*semantics):
    return pltpu.CompilerParams(dimension_semantics=semantics, vmem_limit_bytes=VMEM_LIMIT_BYTES)


def _rms_norm(x, g):
    return x * lax.rsqrt(jnp.mean(x * x, axis=-1, keepdims=True) + RMS_EPS) * g


def _dot(a, b):
    return jnp.dot(a, b, preferred_element_type=F32)


def _dot_nt(a, b):
    return lax.dot_general(a, b, (((1,), (1,)), ((), ())), preferred_element_type=F32)


def _ffn_kernel(x_ref, gin_ref, wg_ref, wu_ref, wd_ref, gout_ref, *rest, emit_residual, row_chunk):
    if emit_residual:
        y_ref, n_ref, h_ref, acc_ref = rest
    else:
        n_ref, h_ref, acc_ref = rest
    f = pl.program_id(1)

    @pl.when(f == 0)
    def _():
        h_ref[...] = _rms_norm(x_ref[...], gin_ref[...]).astype(BF16)
        acc_ref[...] = jnp.zeros_like(acc_ref)

    for r0 in range(0, x_ref.shape[0], row_chunk):
        rows = pl.ds(r0, row_chunk)
        h = h_ref[rows, :]
        g = _dot(h, wg_ref[...])
        u = _dot(h, wu_ref[...])
        a = (g * jax.nn.sigmoid(g) * u).astype(BF16)
        acc_ref[rows, :] += _dot(a, wd_ref[...])

    @pl.when(f == pl.num_programs(1) - 1)
    def _():
        y = x_ref[...] + 0.5 * acc_ref[...]
        if emit_residual:
            y_ref[...] = y
        n_ref[...] = _rms_norm(y, gout_ref[...]).astype(n_ref.dtype)


def _ffn(x, g_in, wg, wu, wd, g_out, *, emit_residual, tm, tf, row_chunk, name):
    t, d = x.shape
    d_ff = wg.shape[1]
    row = pl.BlockSpec((tm, d), lambda i, f: (i, 0), pipeline_mode=pl.Buffered(1))
    vec = pl.BlockSpec((1, d), lambda i, f: (0, 0))
    if emit_residual:
        out_shape = (jax.ShapeDtypeStruct((t, d), F32), jax.ShapeDtypeStruct((t, d), BF16))
        out_specs = (row, row)
    else:
        out_shape = jax.ShapeDtypeStruct((t, d), F32)
        out_specs = row
    return pl.pallas_call(
        functools.partial(_ffn_kernel, emit_residual=emit_residual, row_chunk=row_chunk),
        out_shape=out_shape,
        grid=(t // tm, d_ff // tf),
        in_specs=[
            row, vec,
            pl.BlockSpec((d, tf), lambda i, f: (0, f)),
            pl.BlockSpec((d, tf), lambda i, f: (0, f)),
            pl.BlockSpec((tf, d), lambda i, f: (f, 0)),
            vec,
        ],
        out_specs=out_specs,
        scratch_shapes=[pltpu.VMEM((tm, d), BF16), pltpu.VMEM((tm, d), F32)],
        compiler_params=_params("parallel", "arbitrary"),
        name=name,
    )(x, g_in.reshape(1, d), wg, wu, wd, g_out.reshape(1, d))


def _in_proj_kernel(h_ref, wq_ref, wk_ref, wv_ref, wb_ref, wc_ref, wx_ref, wga_ref, wgc_ref,
                    bga_ref, bgc_ref, cw_ref,
                    q_ref, k_ref, v_ref, cp_ref, ga_ref, gc_ref, u_ref, *, row_chunk):
    seq, tn = q_ref.shape
    halo = 8
    u_ref[0:halo, :] = jnp.zeros((halo, tn), F32)
    cw = cw_ref[...]
    for r0 in range(0, seq, row_chunk):
        rows = pl.ds(r0, row_chunk)
        h = h_ref[rows, :]
        q_ref[rows, :] = _dot(h, wq_ref[...]).astype(BF16)
        k_ref[rows, :] = _dot(h, wk_ref[...]).astype(BF16)
        v_ref[rows, :] = _dot(h, wv_ref[...]).astype(BF16)
        ga_ref[rows, :] = jax.nn.sigmoid(_dot(h, wga_ref[...]) + bga_ref[...]).astype(BF16)
        gc_ref[rows, :] = jax.nn.sigmoid(_dot(h, wgc_ref[...]) + bgc_ref[...]).astype(BF16)
        u = _dot(h, wc_ref[...]) * _dot(h, wx_ref[...])
        u_ref[pl.ds(halo + r0, row_chunk), :] = u
        conv = cw[2:3, :] * u
        for back in range(1, CONV_KERNEL):
            conv += cw[2 - back:3 - back, :] * u_ref[pl.ds(halo + r0 - back, row_chunk), :]
        cp_ref[rows, :] = (_dot(h, wb_ref[...]) * conv).astype(BF16)


def _in_proj(hm, w_in, b_gate, conv_w, *, bsz, seq, tn, row_chunk):
    t, d = hm.shape
    nj = d // tn
    assert w_in.shape == (d, N_IN_SEGMENTS * d) and t == bsz * seq

    def w_spec(seg):
        return pl.BlockSpec((d, tn), lambda b, j, seg=seg: (0, seg * nj + j))

    def b_spec(seg):
        return pl.BlockSpec((1, tn), lambda b, j, seg=seg: (0, seg * nj + j))

    tile = pl.BlockSpec((seq, tn), lambda b, j: (b, j))
    out = jax.ShapeDtypeStruct((t, d), BF16)
    return pl.pallas_call(
        functools.partial(_in_proj_kernel, row_chunk=row_chunk),
        out_shape=(out,) * 6,
        grid=(bsz, nj),
        in_specs=[pl.BlockSpec((seq, d), lambda b, j: (b, 0))]
        + [w_spec(s) for s in range(N_IN_SEGMENTS)]
        + [b_spec(0), b_spec(1), pl.BlockSpec((CONV_KERNEL, tn), lambda b, j: (0, j))],
        out_specs=(tile,) * 6,
        scratch_shapes=[pltpu.VMEM((seq + 8, tn), F32)],
        compiler_params=_params("parallel", "arbitrary"),
        name="in_proj",
    )(hm, *([w_in] * N_IN_SEGMENTS), b_gate.reshape(1, 2 * d), b_gate.reshape(1, 2 * d), conv_w)


def _moba_kernel(slopes_ref, q_ref, k_ref, v_ref, o_ref, qa_ref, ka_ref, va_ref):
    seq, hd = q_ref.shape
    blk = MOBA_BLOCK
    nb = seq // blk
    slope = slopes_ref[pl.program_id(1)]
    scale = hd ** -0.5

    q = q_ref[...]
    k = k_ref[...]

    kmean = jnp.mean(k.astype(F32).reshape(nb, blk, hd), axis=1)
    kmean_hi = kmean.astype(BF16)
    kmean_lo = (kmean - kmean_hi.astype(F32)).astype(BF16)
    gate2 = _dot_nt(jnp.concatenate([kmean_hi, kmean_lo], axis=0), q)
    gate = gate2[:nb] + gate2[nb:]
    blk_id = lax.broadcasted_iota(jnp.int32, (nb, seq), 0)
    q_blk = lax.broadcasted_iota(jnp.int32, (nb, seq), 1) // blk
    gate = jnp.where(blk_id < q_blk, gate, -jnp.inf)
    mask_rows = []
    for n in range(nb):
        g_n = gate[n:n + 1, :]
        beats = jnp.where(gate > g_n, 1.0, jnp.where((gate == g_n) & (blk_id < n), 1.0, 0.0))
        rank = jnp.sum(beats, axis=0, keepdims=True)
        q_blk_row = q_blk[0:1, :]
        visible = ((rank < float(MOBA_TOPK)) & (n < q_blk_row)) | (n == q_blk_row)
        mask_rows.append(jnp.where(visible, 0.0, MASK_NEG))
    mask_t = jnp.concatenate(mask_rows + [jnp.zeros((hd - nb, seq), F32)], axis=0)
    mask_cols = mask_t.T.astype(BF16)

    qa_ref[:, :hd] = q
    qa_ref[:, hd:] = mask_cols
    ka_ref[:, :hd] = k
    lane = lax.broadcasted_iota(jnp.int32, (seq, hd), 1)
    key_blk = lax.broadcasted_iota(jnp.int32, (seq, hd), 0) // blk
    ka_ref[:, hd:] = jnp.where(lane == key_blk, 1.0, 0.0).astype(BF16)
    va_ref[:, :hd] = v_ref[...]
    va_ref[:, hd:] = jnp.ones((seq, hd), BF16)

    causal = (lax.broadcasted_iota(jnp.int32, (blk, blk), 0)
              >= lax.broadcasted_iota(jnp.int32, (blk, blk), 1))
    for qb in range(nb):
        n_keys = (qb + 1) * blk
        qa = qa_ref[qb * blk:(qb + 1) * blk, :]
        key_off = lax.broadcasted_iota(jnp.int32, (1, n_keys), 1) - qb * blk
        s = _dot_nt(qa, ka_ref[0:n_keys, :]) * scale + slope * key_off.astype(F32)
        s_own = jnp.where(causal, s[:, qb * blk:], -jnp.inf)
        s = s_own if qb == 0 else jnp.concatenate([s[:, :qb * blk], s_own], axis=1)
        m = jnp.max(s, axis=-1, keepdims=True)
        p = jnp.exp(s - m).astype(BF16)
        acc = _dot(p, va_ref[0:n_keys, :])
        o_ref[qb * blk:(qb + 1) * blk, :] = (acc[:, :hd] / acc[:, hd:]).astype(o_ref.dtype)


def _moba(q, k, v, *, bsz, seq):
    t, d = q.shape
    slopes = jnp.exp2(-8.0 * jnp.arange(1, N_HEADS + 1, dtype=F32) / N_HEADS)
    tile = pl.BlockSpec((seq, HEAD_DIM), lambda b, h, slopes: (b, h))
    return pl.pallas_call(
        _moba_kernel,
        out_shape=jax.ShapeDtypeStruct((t, d), BF16),
        grid_spec=pltpu.PrefetchScalarGridSpec(
            num_scalar_prefetch=1,
            grid=(bsz, N_HEADS),
            in_specs=[tile, tile, tile],
            out_specs=tile,
            scratch_shapes=[
                pltpu.VMEM((seq, 2 * HEAD_DIM), BF16),
                pltpu.VMEM((seq, 2 * HEAD_DIM), BF16),
                pltpu.VMEM((seq, 2 * HEAD_DIM), BF16),
            ],
        ),
        compiler_params=_params("parallel", "arbitrary"),
        name="moba",
    )(slopes, q, k, v)


def _mix_kernel(a_ref, c_ref, wa_ref, wc_ref, ga_ref, gc_ref, m_ref):
    ya = _dot(a_ref[...], wa_ref[...])
    yc = _dot(c_ref[...], wc_ref[...])
    m_ref[...] = (ga_ref[...].astype(F32) * ya + gc_ref[...].astype(F32) * yc).astype(m_ref.dtype)


def _mix(attn, conv, wa, wc, ga, gc, *, tm, tn):
    t, d = attn.shape
    row = pl.BlockSpec((tm, d), lambda i, j: (i, 0))
    col = pl.BlockSpec((d, tn), lambda i, j: (0, j))
    tile = pl.BlockSpec((tm, tn), lambda i, j: (i, j))
    return pl.pallas_call(
        _mix_kernel,
        out_shape=jax.ShapeDtypeStruct((t, d), BF16),
        grid=(t // tm, d // tn),
        in_specs=[row, row, col, col, tile, tile],
        out_specs=tile,
        compiler_params=_params("parallel", "arbitrary"),
        name="mix",
    )(attn, conv, wa, wc, ga, gc)


def _out_proj_kernel(m_ref, w_ref, x_ref, o_ref):
    o_ref[...] = x_ref[...] + _dot(m_ref[...], w_ref[...])


def _out_proj(m, w, x, *, tm, tn):
    t, d = m.shape
    tile = pl.BlockSpec((tm, tn), lambda i, j: (i, j))
    return pl.pallas_call(
        _out_proj_kernel,
        out_shape=jax.ShapeDtypeStruct((t, d), F32),
        grid=(t // tm, d // tn),
        in_specs=[pl.BlockSpec((tm, d), lambda i, j: (i, 0)),
                  pl.BlockSpec((d, tn), lambda i, j: (0, j)), tile],
        out_specs=tile,
        compiler_params=_params("parallel", "arbitrary"),
        name="out_proj",
    )(m, w, x)


def kernel(x, ffn1_norm, ffn1_w_gate, ffn1_w_up, ffn1_w_down, mix_norm, w_in, b_gate, conv_w,
           w_attn_out, w_conv_out, w_out, ffn2_norm, ffn2_w_gate, ffn2_w_up, ffn2_w_down, final_norm):
    bsz, seq, d = x.shape
    depth = ffn1_norm.shape[0]
    assert d == N_HEADS * HEAD_DIM and seq % MOBA_BLOCK == 0 and seq // MOBA_BLOCK <= HEAD_DIM
    xs = x.reshape(bsz * seq, d)
    for l in range(depth):
        last = l == depth - 1
        x1, hm = _ffn(xs, ffn1_norm[l], ffn1_w_gate[l].astype(BF16), ffn1_w_up[l].astype(BF16),
                      ffn1_w_down[l].astype(BF16), mix_norm[l],
                      emit_residual=True, tm=1024, tf=512, row_chunk=512, name="ffn1")
        q, k, v, conv, ga, gc = _in_proj(hm, w_in[l].astype(BF16), b_gate[l], conv_w[l],
                                         bsz=bsz, seq=seq, tn=256, row_chunk=512)
        attn = _moba(q, k, v, bsz=bsz, seq=seq)
        m = _mix(attn, conv, w_attn_out[l].astype(BF16), w_conv_out[l].astype(BF16), ga, gc,
                 tm=1024, tn=512)
        x2 = _out_proj(m, w_out[l].astype(BF16), x1, tm=1024, tn=512)
        if last:
            xs = _ffn(x2, ffn2_norm[l], ffn2_w_gate[l].astype(BF16), ffn2_w_up[l].astype(BF16),
                      ffn2_w_down[l].astype(BF16), final_norm,
                      emit_residual=False, tm=1024, tf=512, row_chunk=512, name="ffn2")
        else:
            one = jnp.ones((d,), F32)
            xs, _ = _ffn(x2, ffn2_norm[l], ffn2_w_gate[l].astype(BF16), ffn2_w_up[l].astype(BF16),
                         ffn2_w_down[l].astype(BF16), one,
                         emit_residual=True, tm=1024, tf=512, row_chunk=512, name="ffn2")
    if depth == 0:
        raise NotImplementedError("depth 0 has no Pallas work")
    return xs.reshape(bsz, seq, d)
```

```python
import functools

import jax
import jax.numpy as jnp
from jax import lax
from jax.experimental import pallas as pl
from jax.experimental.pallas import tpu as pltpu

F32 = jnp.float32
BF16 = jnp.bfloat16

N_HEADS = 16
HEAD_DIM = 128
CONV_KERNEL = 3
MOBA_BLOCK = 256
MOBA_TOPK = 3
RMS_EPS = 1e-6
N_IN_SEGMENTS = 8

V7X_VMEM_BYTES = 64 * 1024 * 1024
VMEM_LIMIT_BYTES = V7X_VMEM_BYTES - 8 * 1024 * 1024

FFN_TOKEN_TILE = 1024
FFN_DFF_TILE = 512
FFN_ROW_CHUNK = 512

MASK_NEG = -1e30


def _params(*semantics):
    return pltpu.CompilerParams(dimension_semantics=semantics, vmem_limit_bytes=VMEM_LIMIT_BYTES)


def _rms_norm(x, g):
    return x * lax.rsqrt(jnp.mean(x * x, axis=-1, keepdims=True) + RMS_EPS) * g


def _dot(a, b):
    return jnp.dot(a, b, preferred_element_type=F32)


def _dot_nt(a, b):
    return lax.dot_general(a, b, (((1,), (1,)), ((), ())), preferred_element_type=F32)


def _ffn_kernel(x_hbm, gin_ref, wg_ref, wu_ref, wd_ref, gout_ref, *rest, emit_residual, row_chunk):
    if emit_residual:
        y_hbm, n_hbm, h_ref, acc_ref, x_buf, y_buf, n_buf, sems = rest
    else:
        n_hbm, h_ref, acc_ref, x_buf, n_buf, sems = rest
        y_hbm = y_buf = None
    tm = x_buf.shape[0]
    i = pl.program_id(0)
    f = pl.program_id(1)
    n_tiles = pl.num_programs(0)

    def tile(ref, t):
        return ref.at[pl.ds(pl.multiple_of(t * tm, tm), tm), :]

    def x_copy(t):
        return pltpu.make_async_copy(tile(x_hbm, t), x_buf, sems.at[0])

    def y_copy(t):
        return pltpu.make_async_copy(y_buf, tile(y_hbm, t), sems.at[1])

    def n_copy(t):
        return pltpu.make_async_copy(n_buf, tile(n_hbm, t), sems.at[2])

    @pl.when(f == 0)
    def _():
        @pl.when(i == 0)
        def _():
            x_copy(0).start()

        x_copy(i).wait()
        x = x_buf[...]
        h_ref[...] = _rms_norm(x, gin_ref[...]).astype(BF16)
        acc_ref[...] = x

        @pl.when(i + 1 < n_tiles)
        def _():
            x_copy(i + 1).start()

    for r0 in range(0, tm, row_chunk):
        rows = pl.ds(r0, row_chunk)
        h = h_ref[rows, :]
        g = _dot(h, wg_ref[...])
        u = _dot(h, wu_ref[...])
        a = (0.5 * g * jax.nn.sigmoid(g) * u).astype(BF16)
        acc_ref[rows, :] += _dot(a, wd_ref[...])

    @pl.when(f == pl.num_programs(1) - 1)
    def _():
        @pl.when(i > 0)
        def _():
            if emit_residual:
                y_copy(i - 1).wait()
            n_copy(i - 1).wait()

        y = acc_ref[...]
        if emit_residual:
            y_buf[...] = y
            y_copy(i).start()
        n_buf[...] = _rms_norm(y, gout_ref[...]).astype(n_buf.dtype)
        n_copy(i).start()

        @pl.when(i == n_tiles - 1)
        def _():
            if emit_residual:
                y_copy(i).wait()
            n_copy(i).wait()


def _ffn(x, g_in, wg, wu, wd, g_out, *, emit_residual, name):
    t, d = x.shape
    d_ff = wg.shape[1]
    tm, tf = FFN_TOKEN_TILE, FFN_DFF_TILE
    hbm = pl.BlockSpec(memory_space=pl.ANY)
    vec = pl.BlockSpec((1, d), lambda i, f: (0, 0))
    n_dtype = BF16 if emit_residual else F32
    n_shape = jax.ShapeDtypeStruct((t, d), n_dtype)
    if emit_residual:
        out_shape = (jax.ShapeDtypeStruct((t, d), F32), n_shape)
        out_specs = (hbm, hbm)
        out_bufs = [pltpu.VMEM((tm, d), F32), pltpu.VMEM((tm, d), n_dtype)]
    else:
        out_shape = n_shape
        out_specs = hbm
        out_bufs = [pltpu.VMEM((tm, d), n_dtype)]
    return pl.pallas_call(
        functools.partial(_ffn_kernel, emit_residual=emit_residual, row_chunk=FFN_ROW_CHUNK),
        out_shape=out_shape,
        grid=(t // tm, d_ff // tf),
        in_specs=[
            hbm, vec,
            pl.BlockSpec((d, tf), lambda i, f: (0, f)),
            pl.BlockSpec((d, tf), lambda i, f: (0, f)),
            pl.BlockSpec((tf, d), lambda i, f: (f, 0)),
            vec,
        ],
        out_specs=out_specs,
        scratch_shapes=[pltpu.VMEM((tm, d), BF16), pltpu.VMEM((tm, d), F32), pltpu.VMEM((tm, d), F32)]
        + out_bufs + [pltpu.SemaphoreType.DMA((3,))],
        compiler_params=_params("arbitrary", "arbitrary"),
        name=name,
    )(x, g_in.reshape(1, d), wg, wu, wd, g_out.reshape(1, d))


def _in_proj_kernel(h_ref, wq_ref, wk_ref, wv_ref, wb_ref, wc_ref, wx_ref, wga_ref, wgc_ref,
                    bga_ref, bgc_ref, cw_ref,
                    q_ref, k_ref, v_ref, cp_ref, ga_ref, gc_ref, u_ref, *, row_chunk):
    seq, tn = q_ref.shape
    halo = 8
    u_ref[0:halo, :] = jnp.zeros((halo, tn), F32)
    cw = cw_ref[...]
    for r0 in range(0, seq, row_chunk):
        rows = pl.ds(r0, row_chunk)
        h = h_ref[rows, :]
        q_ref[rows, :] = _dot(h, wq_ref[...]).astype(BF16)
        k_ref[rows, :] = _dot(h, wk_ref[...]).astype(BF16)
        v_ref[rows, :] = _dot(h, wv_ref[...]).astype(BF16)
        ga_ref[rows, :] = jax.nn.sigmoid(_dot(h, wga_ref[...]) + bga_ref[...]).astype(BF16)
        gc_ref[rows, :] = jax.nn.sigmoid(_dot(h, wgc_ref[...]) + bgc_ref[...]).astype(BF16)
        u = _dot(h, wc_ref[...]) * _dot(h, wx_ref[...])
        u_ref[pl.ds(halo + r0, row_chunk), :] = u
        conv = cw[2:3, :] * u
        for back in range(1, CONV_KERNEL):
            conv += cw[2 - back:3 - back, :] * u_ref[pl.ds(halo + r0 - back, row_chunk), :]
        cp_ref[rows, :] = (_dot(h, wb_ref[...]) * conv).astype(BF16)


def _in_proj(hm, w_in, b_gate, conv_w, *, bsz, seq, tn, row_chunk):
    t, d = hm.shape
    nj = d // tn
    assert w_in.shape == (d, N_IN_SEGMENTS * d) and t == bsz * seq

    def w_spec(seg):
        return pl.BlockSpec((d, tn), lambda b, j, seg=seg: (0, seg * nj + j))

    def b_spec(seg):
        return pl.BlockSpec((1, tn), lambda b, j, seg=seg: (0, seg * nj + j))

    tile = pl.BlockSpec((seq, tn), lambda b, j: (b, j))
    out = jax.ShapeDtypeStruct((t, d), BF16)
    return pl.pallas_call(
        functools.partial(_in_proj_kernel, row_chunk=row_chunk),
        out_shape=(out,) * 6,
        grid=(bsz, nj),
        in_specs=[pl.BlockSpec((seq, d), lambda b, j: (b, 0))]
        + [w_spec(s) for s in range(N_IN_SEGMENTS)]
        + [b_spec(0), b_spec(1), pl.BlockSpec((CONV_KERNEL, tn), lambda b, j: (0, j))],
        out_specs=(tile,) * 6,
        scratch_shapes=[pltpu.VMEM((seq + 8, tn), F32)],
        compiler_params=_params("parallel", "arbitrary"),
        name="in_proj",
    )(hm, *([w_in] * N_IN_SEGMENTS), b_gate.reshape(1, 2 * d), b_gate.reshape(1, 2 * d), conv_w)


def _moba_kernel(slopes_ref, q_ref, k_ref, v_ref, o_ref, qa_ref, ka_ref, va_ref):
    seq, hd = q_ref.shape
    blk = MOBA_BLOCK
    nb = seq // blk
    slope = slopes_ref[pl.program_id(1)]
    scale = hd ** -0.5

    q = q_ref[...]
    k = k_ref[...]

    kmean = jnp.mean(k.astype(F32).reshape(nb, blk, hd), axis=1)
    kmean_hi = kmean.astype(BF16)
    kmean_lo = (kmean - kmean_hi.astype(F32)).astype(BF16)
    gate2 = _dot_nt(jnp.concatenate([kmean_hi, kmean_lo], axis=0), q)
    gate = gate2[:nb] + gate2[nb:]
    blk_id = lax.broadcasted_iota(jnp.int32, (nb, seq), 0)
    q_blk = lax.broadcasted_iota(jnp.int32, (nb, seq), 1) // blk
    gate = jnp.where(blk_id < q_blk, gate, -jnp.inf)
    mask_rows = []
    for n in range(nb):
        g_n = gate[n:n + 1, :]
        beats = jnp.where(gate > g_n, 1.0, jnp.where((gate == g_n) & (blk_id < n), 1.0, 0.0))
        rank = jnp.sum(beats, axis=0, keepdims=True)
        q_blk_row = q_blk[0:1, :]
        visible = ((rank < float(MOBA_TOPK)) & (n < q_blk_row)) | (n == q_blk_row)
        mask_rows.append(jnp.where(visible, 0.0, MASK_NEG))
    mask_t = jnp.concatenate(mask_rows + [jnp.zeros((hd - nb, seq), F32)], axis=0)
    mask_cols = mask_t.T.astype(BF16)

    qa_ref[:, :hd] = q
    qa_ref[:, hd:] = mask_cols
    ka_ref[:, :hd] = k
    lane = lax.broadcasted_iota(jnp.int32, (seq, hd), 1)
    key_blk = lax.broadcasted_iota(jnp.int32, (seq, hd), 0) // blk
    ka_ref[:, hd:] = jnp.where(lane == key_blk, 1.0, 0.0).astype(BF16)
    va_ref[:, :hd] = v_ref[...]
    va_ref[:, hd:] = jnp.ones((seq, hd), BF16)

    causal = (lax.broadcasted_iota(jnp.int32, (blk, blk), 0)
              >= lax.broadcasted_iota(jnp.int32, (blk, blk), 1))
    for qb in range(nb):
        n_keys = (qb + 1) * blk
        qa = qa_ref[qb * blk:(qb + 1) * blk, :]
        key_off = lax.broadcasted_iota(jnp.int32, (1, n_keys), 1) - qb * blk
        s = _dot_nt(qa, ka_ref[0:n_keys, :]) * scale + slope * key_off.astype(F32)
        s_own = jnp.where(causal, s[:, qb * blk:], -jnp.inf)
        s = s_own if qb == 0 else jnp.concatenate([s[:, :qb * blk], s_own], axis=1)
        m = jnp.max(s, axis=-1, keepdims=True)
        p = jnp.exp(s - m).astype(BF16)
        acc = _dot(p, va_ref[0:n_keys, :])
        o_ref[qb * blk:(qb + 1) * blk, :] = (acc[:, :hd] / acc[:, hd:]).astype(o_ref.dtype)


def _moba(q, k, v, *, bsz, seq):
    t, d = q.shape
    slopes = jnp.exp2(-8.0 * jnp.arange(1, N_HEADS + 1, dtype=F32) / N_HEADS)
    tile = pl.BlockSpec((seq, HEAD_DIM), lambda b, h, slopes: (b, h))
    return pl.pallas_call(
        _moba_kernel,
        out_shape=jax.ShapeDtypeStruct((t, d), BF16),
        grid_spec=pltpu.PrefetchScalarGridSpec(
            num_scalar_prefetch=1,
            grid=(bsz, N_HEADS),
            in_specs=[tile, tile, tile],
            out_specs=tile,
            scratch_shapes=[
                pltpu.VMEM((seq, 2 * HEAD_DIM), BF16),
                pltpu.VMEM((seq, 2 * HEAD_DIM), BF16),
                pltpu.VMEM((seq, 2 * HEAD_DIM), BF16),
            ],
        ),
        compiler_params=_params("parallel", "arbitrary"),
        name="moba",
    )(slopes, q, k, v)


def _mix_kernel(a_ref, c_ref, wa_ref, wc_ref, ga_ref, gc_ref, m_ref):
    ya = _dot(a_ref[...], wa_ref[...])
    yc = _dot(c_ref[...], wc_ref[...])
    m_ref[...] = (ga_ref[...].astype(F32) * ya + gc_ref[...].astype(F32) * yc).astype(m_ref.dtype)


def _mix(attn, conv, wa, wc, ga, gc, *, tm, tn):
    t, d = attn.shape
    row = pl.BlockSpec((tm, d), lambda i, j: (i, 0))
    col = pl.BlockSpec((d, tn), lambda i, j: (0, j))
    tile = pl.BlockSpec((tm, tn), lambda i, j: (i, j))
    return pl.pallas_call(
        _mix_kernel,
        out_shape=jax.ShapeDtypeStruct((t, d), BF16),
        grid=(t // tm, d // tn),
        in_specs=[row, row, col, col, tile, tile],
        out_specs=tile,
        compiler_params=_params("parallel", "arbitrary"),
        name="mix",
    )(attn, conv, wa, wc, ga, gc)


def _out_proj_kernel(m_ref, w_ref, x_ref, o_ref):
    o_ref[...] = x_ref[...] + _dot(m_ref[...], w_ref[...])


def _out_proj(m, w, x, *, tm, tn):
    t, d = m.shape
    tile = pl.BlockSpec((tm, tn), lambda i, j: (i, j))
    return pl.pallas_call(
        _out_proj_kernel,
        out_shape=jax.ShapeDtypeStruct((t, d), F32),
        grid=(t // tm, d // tn),
        in_specs=[pl.BlockSpec((tm, d), lambda i, j: (i, 0)),
                  pl.BlockSpec((d, tn), lambda i, j: (0, j)), tile],
        out_specs=tile,
        compiler_params=_params("parallel", "arbitrary"),
        name="out_proj",
    )(m, w, x)


def kernel(x, ffn1_norm, ffn1_w_gate, ffn1_w_up, ffn1_w_down, mix_norm, w_in, b_gate, conv_w,
           w_attn_out, w_conv_out, w_out, ffn2_norm, ffn2_w_gate, ffn2_w_up, ffn2_w_down, final_norm):
    bsz, seq, d = x.shape
    depth = ffn1_norm.shape[0]
    assert d == N_HEADS * HEAD_DIM and seq % MOBA_BLOCK == 0 and seq // MOBA_BLOCK <= HEAD_DIM
    xs = x.reshape(bsz * seq, d)
    for l in range(depth):
        last = l == depth - 1
        x1, hm = _ffn(xs, ffn1_norm[l], ffn1_w_gate[l].astype(BF16), ffn1_w_up[l].astype(BF16),
                      ffn1_w_down[l].astype(BF16), mix_norm[l],
                      emit_residual=True, name="ffn1")
        q, k, v, conv, ga, gc = _in_proj(hm, w_in[l].astype(BF16), b_gate[l], conv_w[l],
                                         bsz=bsz, seq=seq, tn=256, row_chunk=512)
        attn = _moba(q, k, v, bsz=bsz, seq=seq)
        m = _mix(attn, conv, w_attn_out[l].astype(BF16), w_conv_out[l].astype(BF16), ga, gc,
                 tm=1024, tn=512)
        x2 = _out_proj(m, w_out[l].astype(BF16), x1, tm=1024, tn=512)
        if last:
            xs = _ffn(x2, ffn2_norm[l], ffn2_w_gate[l].astype(BF16), ffn2_w_up[l].astype(BF16),
                      ffn2_w_down[l].astype(BF16), final_norm,
                      emit_residual=False, name="ffn2")
        else:
            one = jnp.ones((d,), F32)
            xs, _ = _ffn(x2, ffn2_norm[l], ffn2_w_gate[l].astype(BF16), ffn2_w_up[l].astype(BF16),
                         ffn2_w_down[l].astype(BF16), one,
                         emit_residual=True, name="ffn2")
    return xs.reshape(bsz, seq, d)
```

```python
import functools

import jax
import jax.numpy as jnp
from jax import lax
from jax.experimental import pallas as pl
from jax.experimental.pallas import tpu as pltpu

F32 = jnp.float32
BF16 = jnp.bfloat16

N_HEADS = 16
HEAD_DIM = 128
CONV_KERNEL = 3
MOBA_BLOCK = 256
MOBA_TOPK = 3
ALIBI_PARTS = 3
RMS_EPS = 1e-6
N_IN_SEGMENTS = 8

V7X_VMEM_BYTES = 64 * 1024 * 1024
VMEM_LIMIT_BYTES = V7X_VMEM_BYTES - 8 * 1024 * 1024

FFN_TOKEN_TILE = 1024
FFN_DFF_TILE = 512
FFN_ROW_CHUNK = 512

LOG2_E = 1.4426950408889634
QK_LOG2_SCALE = LOG2_E * HEAD_DIM ** -0.5

MASK_NEG = -1e30


def _params(*semantics):
    return pltpu.CompilerParams(dimension_semantics=semantics, vmem_limit_bytes=VMEM_LIMIT_BYTES)


def _rms_norm(x, g):
    return x * lax.rsqrt(jnp.mean(x * x, axis=-1, keepdims=True) + RMS_EPS) * g


def _dot(a, b):
    return jnp.dot(a, b, preferred_element_type=F32)


def _dot_nt(a, b):
    return lax.dot_general(a, b, (((1,), (1,)), ((), ())), preferred_element_type=F32)


def _ffn_kernel(x_hbm, gin_ref, wg_ref, wu_ref, wd_ref, gout_ref, *rest, emit_residual, row_chunk):
    if emit_residual:
        y_hbm, n_hbm, h_ref, acc_ref, x_buf, y_buf, n_buf, sems = rest
    else:
        n_hbm, h_ref, acc_ref, x_buf, n_buf, sems = rest
        y_hbm = y_buf = None
    tm = x_buf.shape[0]
    i = pl.program_id(0)
    f = pl.program_id(1)
    n_tiles = pl.num_programs(0)

    def tile(ref, t):
        return ref.at[pl.ds(pl.multiple_of(t * tm, tm), tm), :]

    def x_copy(t):
        return pltpu.make_async_copy(tile(x_hbm, t), x_buf, sems.at[0])

    def y_copy(t):
        return pltpu.make_async_copy(y_buf, tile(y_hbm, t), sems.at[1])

    def n_copy(t):
        return pltpu.make_async_copy(n_buf, tile(n_hbm, t), sems.at[2])

    @pl.when(f == 0)
    def _():
        @pl.when(i == 0)
        def _():
            x_copy(0).start()

        x_copy(i).wait()
        x = x_buf[...]
        h_ref[...] = _rms_norm(x, gin_ref[...]).astype(BF16)
        acc_ref[...] = x

        @pl.when(i + 1 < n_tiles)
        def _():
            x_copy(i + 1).start()

    for r0 in range(0, tm, row_chunk):
        rows = pl.ds(r0, row_chunk)
        h = h_ref[rows, :]
        g = _dot(h, wg_ref[...])
        u = _dot(h, wu_ref[...])
        a = (0.5 * g * jax.nn.sigmoid(g) * u).astype(BF16)
        acc_ref[rows, :] += _dot(a, wd_ref[...])

    @pl.when(f == pl.num_programs(1) - 1)
    def _():
        @pl.when(i > 0)
        def _():
            if emit_residual:
                y_copy(i - 1).wait()
            n_copy(i - 1).wait()

        y = acc_ref[...]
        if emit_residual:
            y_buf[...] = y
            y_copy(i).start()
        n_buf[...] = _rms_norm(y, gout_ref[...]).astype(n_buf.dtype)
        n_copy(i).start()

        @pl.when(i == n_tiles - 1)
        def _():
            if emit_residual:
                y_copy(i).wait()
            n_copy(i).wait()


def _ffn(x, g_in, wg, wu, wd, g_out, *, emit_residual, name):
    t, d = x.shape
    d_ff = wg.shape[1]
    tm, tf = FFN_TOKEN_TILE, FFN_DFF_TILE
    hbm = pl.BlockSpec(memory_space=pl.ANY)
    vec = pl.BlockSpec((1, d), lambda i, f: (0, 0))
    n_dtype = BF16 if emit_residual else F32
    n_shape = jax.ShapeDtypeStruct((t, d), n_dtype)
    if emit_residual:
        out_shape = (jax.ShapeDtypeStruct((t, d), F32), n_shape)
        out_specs = (hbm, hbm)
        out_bufs = [pltpu.VMEM((tm, d), F32), pltpu.VMEM((tm, d), n_dtype)]
    else:
        out_shape = n_shape
        out_specs = hbm
        out_bufs = [pltpu.VMEM((tm, d), n_dtype)]
    return pl.pallas_call(
        functools.partial(_ffn_kernel, emit_residual=emit_residual, row_chunk=FFN_ROW_CHUNK),
        out_shape=out_shape,
        grid=(t // tm, d_ff // tf),
        in_specs=[
            hbm, vec,
            pl.BlockSpec((d, tf), lambda i, f: (0, f)),
            pl.BlockSpec((d, tf), lambda i, f: (0, f)),
            pl.BlockSpec((tf, d), lambda i, f: (f, 0)),
            vec,
        ],
        out_specs=out_specs,
        scratch_shapes=[pltpu.VMEM((tm, d), BF16), pltpu.VMEM((tm, d), F32), pltpu.VMEM((tm, d), F32)]
        + out_bufs + [pltpu.SemaphoreType.DMA((3,))],
        compiler_params=_params("arbitrary", "arbitrary"),
        name=name,
    )(x, g_in.reshape(1, d), wg, wu, wd, g_out.reshape(1, d))


def _in_proj_kernel(h_ref, wq_ref, wk_ref, wv_ref, wb_ref, wc_ref, wx_ref, wga_ref, wgc_ref,
                    bga_ref, bgc_ref, cw_ref,
                    q_ref, k_ref, v_ref, cp_ref, ga_ref, gc_ref, u_ref, *, row_chunk):
    seq, tn = q_ref.shape
    halo = 8
    u_ref[0:halo, :] = jnp.zeros((halo, tn), F32)
    cw = cw_ref[...]
    for r0 in range(0, seq, row_chunk):
        rows = pl.ds(r0, row_chunk)
        h = h_ref[rows, :]
        q_ref[rows, :] = (_dot(h, wq_ref[...]) * QK_LOG2_SCALE).astype(BF16)
        k_ref[rows, :] = _dot(h, wk_ref[...]).astype(BF16)
        v_ref[rows, :] = _dot(h, wv_ref[...]).astype(BF16)
        ga_ref[rows, :] = jax.nn.sigmoid(_dot(h, wga_ref[...]) + bga_ref[...]).astype(BF16)
        gc_ref[rows, :] = jax.nn.sigmoid(_dot(h, wgc_ref[...]) + bgc_ref[...]).astype(BF16)
        u = _dot(h, wc_ref[...]) * _dot(h, wx_ref[...])
        u_ref[pl.ds(halo + r0, row_chunk), :] = u
        conv = cw[2:3, :] * u
        for back in range(1, CONV_KERNEL):
            conv += cw[2 - back:3 - back, :] * u_ref[pl.ds(halo + r0 - back, row_chunk), :]
        cp_ref[rows, :] = (_dot(h, wb_ref[...]) * conv).astype(BF16)


def _in_proj(hm, w_in, b_gate, conv_w, *, bsz, seq, tn, row_chunk):
    t, d = hm.shape
    nj = d // tn
    assert w_in.shape == (d, N_IN_SEGMENTS * d) and t == bsz * seq

    def w_spec(seg):
        return pl.BlockSpec((d, tn), lambda b, j, seg=seg: (0, seg * nj + j))

    def b_spec(seg):
        return pl.BlockSpec((1, tn), lambda b, j, seg=seg: (0, seg * nj + j))

    tile = pl.BlockSpec((seq, tn), lambda b, j: (b, j))
    out = jax.ShapeDtypeStruct((t, d), BF16)
    return pl.pallas_call(
        functools.partial(_in_proj_kernel, row_chunk=row_chunk),
        out_shape=(out,) * 6,
        grid=(bsz, nj),
        in_specs=[pl.BlockSpec((seq, d), lambda b, j: (b, 0))]
        + [w_spec(s) for s in range(N_IN_SEGMENTS)]
        + [b_spec(0), b_spec(1), pl.BlockSpec((CONV_KERNEL, tn), lambda b, j: (0, j))],
        out_specs=(tile,) * 6,
        scratch_shapes=[pltpu.VMEM((seq + 8, tn), F32)],
        compiler_params=_params("parallel", "arbitrary"),
        name="in_proj",
    )(hm, *([w_in] * N_IN_SEGMENTS), b_gate.reshape(1, 2 * d), b_gate.reshape(1, 2 * d), conv_w)


def _moba_kernel(kaug_ref, q_ref, k_ref, v_ref, o_ref, qa_ref, ka_ref, va_ref):
    seq, hd = q_ref.shape
    blk = MOBA_BLOCK
    nb = seq // blk

    q = q_ref[...]
    k = k_ref[...]

    kmean = jnp.mean(k.astype(F32).reshape(nb, blk, hd), axis=1)
    kmean_hi = kmean.astype(BF16)
    kmean_lo = (kmean - kmean_hi.astype(F32)).astype(BF16)
    gate2 = _dot_nt(jnp.concatenate([kmean_hi, kmean_lo], axis=0), q)
    gate = gate2[:nb] + gate2[nb:]
    blk_id = lax.broadcasted_iota(jnp.int32, (nb, seq), 0)
    q_blk = lax.broadcasted_iota(jnp.int32, (nb, seq), 1) // blk
    gate = jnp.where(blk_id < q_blk, gate, -jnp.inf)
    mask_rows = []
    for n in range(nb):
        g_n = gate[n:n + 1, :]
        beats = jnp.where(gate > g_n, 1.0, jnp.where((gate == g_n) & (blk_id < n), 1.0, 0.0))
        rank = jnp.sum(beats, axis=0, keepdims=True)
        q_blk_row = q_blk[0:1, :]
        visible = ((rank < float(MOBA_TOPK)) & (n < q_blk_row)) | (n == q_blk_row)
        mask_rows.append(jnp.where(visible, 0.0, MASK_NEG))
    mask_t = jnp.concatenate(mask_rows + [jnp.ones((ALIBI_PARTS, seq), F32),
                                          jnp.zeros((hd - nb - ALIBI_PARTS, seq), F32)], axis=0)
    mask_cols = mask_t.T.astype(BF16)

    qa_ref[:, :hd] = q
    qa_ref[:, hd:] = mask_cols
    ka_ref[:, :hd] = k
    ka_ref[:, hd:] = kaug_ref[...]
    va_ref[:, :hd] = v_ref[...]
    va_ref[:, hd:] = jnp.ones((seq, hd), BF16)

    causal = (lax.broadcasted_iota(jnp.int32, (blk, blk), 0)
              >= lax.broadcasted_iota(jnp.int32, (blk, blk), 1))

    def scores(qb):
        return _dot_nt(qa_ref[qb * blk:(qb + 1) * blk, :], ka_ref[0:(qb + 1) * blk, :])

    order = list(range(nb - 1, -1, -1))
    s_next = scores(order[0])
    for idx, qb in enumerate(order):
        s = s_next
        if idx + 1 < nb:
            s_next = scores(order[idx + 1])
        s_own = jnp.where(causal, s[:, qb * blk:], -jnp.inf)
        s = s_own if qb == 0 else jnp.concatenate([s[:, :qb * blk], s_own], axis=1)
        m = jnp.max(s, axis=-1, keepdims=True)
        p = jnp.exp2(s - m).astype(BF16)
        acc = _dot(p, va_ref[0:(qb + 1) * blk, :])
        o_ref[qb * blk:(qb + 1) * blk, :] = (acc[:, :hd] / acc[:, hd:]).astype(o_ref.dtype)


def _key_augmentation(seq):
    nb = seq // MOBA_BLOCK
    slopes = jnp.exp2(-8.0 * jnp.arange(1, N_HEADS + 1, dtype=F32) / N_HEADS)
    t_k = jnp.arange(seq, dtype=F32)
    bias = LOG2_E * slopes[:, None] * t_k[None, :]
    parts = []
    rest = bias
    for _ in range(ALIBI_PARTS):
        top = lax.bitcast_convert_type(
            lax.bitcast_convert_type(rest, jnp.uint32) & jnp.uint32(0xFFFF0000), F32)
        parts.append(top.astype(BF16))
        rest = rest - top
    block_cols = (jnp.arange(seq)[:, None] // MOBA_BLOCK == jnp.arange(nb)[None, :]).astype(BF16)
    block_cols = jnp.broadcast_to(block_cols[None], (N_HEADS, seq, nb))
    pad = jnp.zeros((N_HEADS, seq, HEAD_DIM - nb - ALIBI_PARTS), BF16)
    return jnp.concatenate([block_cols] + [p[:, :, None] for p in parts] + [pad], axis=-1)


def _moba(q, k, v, *, bsz, seq):
    t, d = q.shape
    tile = pl.BlockSpec((seq, HEAD_DIM), lambda b, h: (b, h))
    return pl.pallas_call(
        _moba_kernel,
        out_shape=jax.ShapeDtypeStruct((t, d), BF16),
        grid=(bsz, N_HEADS),
        in_specs=[pl.BlockSpec((None, seq, HEAD_DIM), lambda b, h: (h, 0, 0)), tile, tile, tile],
        out_specs=tile,
        scratch_shapes=[
            pltpu.VMEM((seq, 2 * HEAD_DIM), BF16),
            pltpu.VMEM((seq, 2 * HEAD_DIM), BF16),
            pltpu.VMEM((seq, 2 * HEAD_DIM), BF16),
        ],
        compiler_params=_params("parallel", "arbitrary"),
        name="moba",
    )(_key_augmentation(seq), q, k, v)


def _mix_kernel(a_ref, c_ref, wa_ref, wc_ref, ga_ref, gc_ref, m_ref):
    ya = _dot(a_ref[...], wa_ref[...])
    yc = _dot(c_ref[...], wc_ref[...])
    m_ref[...] = (ga_ref[...].astype(F32) * ya + gc_ref[...].astype(F32) * yc).astype(m_ref.dtype)


def _mix(attn, conv, wa, wc, ga, gc, *, tm, tn):
    t, d = attn.shape
    row = pl.BlockSpec((tm, d), lambda i, j: (i, 0))
    col = pl.BlockSpec((d, tn), lambda i, j: (0, j))
    tile = pl.BlockSpec((tm, tn), lambda i, j: (i, j))
    return pl.pallas_call(
        _mix_kernel,
        out_shape=jax.ShapeDtypeStruct((t, d), BF16),
        grid=(t // tm, d // tn),
        in_specs=[row, row, col, col, tile, tile],
        out_specs=tile,
        compiler_params=_params("parallel", "arbitrary"),
        name="mix",
    )(attn, conv, wa, wc, ga, gc)


def _out_proj_kernel(m_ref, w_ref, x_ref, o_ref):
    o_ref[...] = x_ref[...] + _dot(m_ref[...], w_ref[...])


def _out_proj(m, w, x, *, tm, tn):
    t, d = m.shape
    tile = pl.BlockSpec((tm, tn), lambda i, j: (i, j))
    return pl.pallas_call(
        _out_proj_kernel,
        out_shape=jax.ShapeDtypeStruct((t, d), F32),
        grid=(t // tm, d // tn),
        in_specs=[pl.BlockSpec((tm, d), lambda i, j: (i, 0)),
                  pl.BlockSpec((d, tn), lambda i, j: (0, j)), tile],
        out_specs=tile,
        compiler_params=_params("parallel", "arbitrary"),
        name="out_proj",
    )(m, w, x)


def kernel(x, ffn1_norm, ffn1_w_gate, ffn1_w_up, ffn1_w_down, mix_norm, w_in, b_gate, conv_w,
           w_attn_out, w_conv_out, w_out, ffn2_norm, ffn2_w_gate, ffn2_w_up, ffn2_w_down, final_norm):
    bsz, seq, d = x.shape
    depth = ffn1_norm.shape[0]
    assert d == N_HEADS * HEAD_DIM and seq % MOBA_BLOCK == 0 and seq // MOBA_BLOCK <= HEAD_DIM
    xs = x.reshape(bsz * seq, d)
    for l in range(depth):
        last = l == depth - 1
        x1, hm = _ffn(xs, ffn1_norm[l], ffn1_w_gate[l].astype(BF16), ffn1_w_up[l].astype(BF16),
                      ffn1_w_down[l].astype(BF16), mix_norm[l],
                      emit_residual=True, name="ffn1")
        q, k, v, conv, ga, gc = _in_proj(hm, w_in[l].astype(BF16), b_gate[l], conv_w[l],
                                         bsz=bsz, seq=seq, tn=256, row_chunk=512)
        attn = _moba(q, k, v, bsz=bsz, seq=seq)
        m = _mix(attn, conv, w_attn_out[l].astype(BF16), w_conv_out[l].astype(BF16), ga, gc,
                 tm=1024, tn=512)
        x2 = _out_proj(m, w_out[l].astype(BF16), x1, tm=1024, tn=512)
        if last:
            xs = _ffn(x2, ffn2_norm[l], ffn2_w_gate[l].astype(BF16), ffn2_w_up[l].astype(BF16),
                      ffn2_w_down[l].astype(BF16), final_norm,
                      emit_residual=False, name="ffn2")
        else:
            one = jnp.ones((d,), F32)
            xs, _ = _ffn(x2, ffn2_norm[l], ffn2_w_gate[l].astype(BF16), ffn2_w_up[l].astype(BF16),
                         ffn2_w_down[l].astype(BF16), one,
                         emit_residual=True, name="ffn2")
    return xs.reshape(bsz, seq, d)
```

```python
import functools

import jax
import jax.numpy as jnp
from jax import lax
from jax.experimental import pallas as pl
from jax.experimental.pallas import tpu as pltpu

F32 = jnp.float32
BF16 = jnp.bfloat16

N_HEADS = 16
HEAD_DIM = 128
CONV_KERNEL = 3
MOBA_BLOCK = 256
MOBA_TOPK = 3
ALIBI_PARTS = 3
RMS_EPS = 1e-6
N_IN_SEGMENTS = 8

V7X_VMEM_BYTES = 64 * 1024 * 1024
VMEM_LIMIT_BYTES = V7X_VMEM_BYTES - 8 * 1024 * 1024

FFN_TOKEN_TILE = 1024
FFN_DFF_TILE = 512
FFN_ROW_CHUNK = 512

IN_PROJ_CHANNEL_TILE = 128
IN_PROJ_ROW_CHUNK = 512

MIX_TOKEN_TILE = 512
MIX_CHANNEL_TILE = 512
MIX_ROW_CHUNK = 256

LOG2_E = 1.4426950408889634
QK_LOG2_SCALE = LOG2_E * HEAD_DIM ** -0.5

MASK_NEG = -1e30


def _params(*semantics):
    return pltpu.CompilerParams(dimension_semantics=semantics, vmem_limit_bytes=VMEM_LIMIT_BYTES)


def _rms_norm(x, g):
    return x * lax.rsqrt(jnp.mean(x * x, axis=-1, keepdims=True) + RMS_EPS) * g


def _dot(a, b):
    return jnp.dot(a, b, preferred_element_type=F32)


def _dot_nt(a, b):
    return lax.dot_general(a, b, (((1,), (1,)), ((), ())), preferred_element_type=F32)


def _ffn_kernel(x_hbm, gin_ref, wg_ref, wu_ref, wd_ref, gout_ref, *rest, emit_residual, row_chunk):
    if emit_residual:
        y_hbm, n_hbm, h_ref, acc_ref, x_buf, y_buf, n_buf, sems = rest
    else:
        n_hbm, h_ref, acc_ref, x_buf, n_buf, sems = rest
        y_hbm = y_buf = None
    tm = x_buf.shape[0]
    i = pl.program_id(0)
    f = pl.program_id(1)
    n_tiles = pl.num_programs(0)

    def tile(ref, t):
        return ref.at[pl.ds(pl.multiple_of(t * tm, tm), tm), :]

    def x_copy(t):
        return pltpu.make_async_copy(tile(x_hbm, t), x_buf, sems.at[0])

    def y_copy(t):
        return pltpu.make_async_copy(y_buf, tile(y_hbm, t), sems.at[1])

    def n_copy(t):
        return pltpu.make_async_copy(n_buf, tile(n_hbm, t), sems.at[2])

    @pl.when(f == 0)
    def _():
        @pl.when(i == 0)
        def _():
            x_copy(0).start()

        x_copy(i).wait()
        x = x_buf[...]
        h_ref[...] = _rms_norm(x, gin_ref[...]).astype(BF16)
        acc_ref[...] = x

        @pl.when(i + 1 < n_tiles)
        def _():
            x_copy(i + 1).start()

    for r0 in range(0, tm, row_chunk):
        rows = pl.ds(r0, row_chunk)
        h = h_ref[rows, :]
        g = _dot(h, wg_ref[...])
        u = _dot(h, wu_ref[...])
        a = (0.5 * g * jax.nn.sigmoid(g) * u).astype(BF16)
        acc_ref[rows, :] += _dot(a, wd_ref[...])

    @pl.when(f == pl.num_programs(1) - 1)
    def _():
        @pl.when(i > 0)
        def _():
            if emit_residual:
                y_copy(i - 1).wait()
            n_copy(i - 1).wait()

        y = acc_ref[...]
        if emit_residual:
            y_buf[...] = y
            y_copy(i).start()
        n_buf[...] = _rms_norm(y, gout_ref[...]).astype(n_buf.dtype)
        n_copy(i).start()

        @pl.when(i == n_tiles - 1)
        def _():
            if emit_residual:
                y_copy(i).wait()
            n_copy(i).wait()


def _ffn(x, g_in, wg, wu, wd, g_out, *, emit_residual, name):
    t, d = x.shape
    d_ff = wg.shape[1]
    tm, tf = FFN_TOKEN_TILE, FFN_DFF_TILE
    hbm = pl.BlockSpec(memory_space=pl.ANY)
    vec = pl.BlockSpec((1, d), lambda i, f: (0, 0))
    n_dtype = BF16 if emit_residual else F32
    n_shape = jax.ShapeDtypeStruct((t, d), n_dtype)
    if emit_residual:
        out_shape = (jax.ShapeDtypeStruct((t, d), F32), n_shape)
        out_specs = (hbm, hbm)
        out_bufs = [pltpu.VMEM((tm, d), F32), pltpu.VMEM((tm, d), n_dtype)]
    else:
        out_shape = n_shape
        out_specs = hbm
        out_bufs = [pltpu.VMEM((tm, d), n_dtype)]
    return pl.pallas_call(
        functools.partial(_ffn_kernel, emit_residual=emit_residual, row_chunk=FFN_ROW_CHUNK),
        out_shape=out_shape,
        grid=(t // tm, d_ff // tf),
        in_specs=[
            hbm, vec,
            pl.BlockSpec((d, tf), lambda i, f: (0, f)),
            pl.BlockSpec((d, tf), lambda i, f: (0, f)),
            pl.BlockSpec((tf, d), lambda i, f: (f, 0)),
            vec,
        ],
        out_specs=out_specs,
        scratch_shapes=[pltpu.VMEM((tm, d), BF16), pltpu.VMEM((tm, d), F32), pltpu.VMEM((tm, d), F32)]
        + out_bufs + [pltpu.SemaphoreType.DMA((3,))],
        compiler_params=_params("arbitrary", "arbitrary"),
        name=name,
    )(x, g_in.reshape(1, d), wg, wu, wd, g_out.reshape(1, d))


def _in_proj_kernel(h_ref, *refs, row_chunk):
    w_refs = refs[:N_IN_SEGMENTS]
    bga_ref, bgc_ref, cw_ref, q_ref, k_ref, v_ref, cp_ref, ga_ref, gc_ref, wcat_ref, u_ref = refs[N_IN_SEGMENTS:]
    seq, tn = q_ref.shape

    @pl.when(pl.program_id(1) == 0)
    def _():
        for seg, w_ref in enumerate(w_refs):
            wcat_ref[:, seg * tn:(seg + 1) * tn] = w_ref[...].astype(BF16)

    halo = 8
    u_ref[0:halo, :] = jnp.zeros((halo, tn), F32)
    cw = cw_ref[...]
    for r0 in range(0, seq, row_chunk):
        rows = pl.ds(r0, row_chunk)
        z = _dot(h_ref[rows, :], wcat_ref[...])
        zq, zk, zv, zb, zc, zx, zga, zgc = [z[:, seg * tn:(seg + 1) * tn] for seg in range(N_IN_SEGMENTS)]
        q_ref[rows, :] = (zq * QK_LOG2_SCALE).astype(BF16)
        k_ref[rows, :] = zk.astype(BF16)
        v_ref[rows, :] = zv.astype(BF16)
        ga_ref[rows, :] = jax.nn.sigmoid(zga + bga_ref[...]).astype(BF16)
        gc_ref[rows, :] = jax.nn.sigmoid(zgc + bgc_ref[...]).astype(BF16)
        u = zc * zx
        u_ref[pl.ds(halo + r0, row_chunk), :] = u
        conv = cw[2:3, :] * u
        for back in range(1, CONV_KERNEL):
            conv += cw[2 - back:3 - back, :] * u_ref[pl.ds(halo + r0 - back, row_chunk), :]
        cp_ref[rows, :] = (zb * conv).astype(BF16)


def _in_proj(hm, w_in, b_gate, conv_w, *, bsz, seq):
    t, d = hm.shape
    tn = IN_PROJ_CHANNEL_TILE
    nj = d // tn
    assert w_in.shape == (d, N_IN_SEGMENTS * d) and t == bsz * seq

    def w_spec(seg):
        return pl.BlockSpec((d, tn), lambda j, b, seg=seg: (0, seg * nj + j))

    def b_spec(seg):
        return pl.BlockSpec((1, tn), lambda j, b, seg=seg: (0, seg * nj + j))

    tile = pl.BlockSpec((seq, tn), lambda j, b: (b, j))
    out = jax.ShapeDtypeStruct((t, d), BF16)
    return pl.pallas_call(
        functools.partial(_in_proj_kernel, row_chunk=IN_PROJ_ROW_CHUNK),
        out_shape=(out,) * 6,
        grid=(nj, bsz),
        in_specs=[pl.BlockSpec((seq, d), lambda j, b: (b, 0))]
        + [w_spec(s) for s in range(N_IN_SEGMENTS)]
        + [b_spec(0), b_spec(1), pl.BlockSpec((CONV_KERNEL, tn), lambda j, b: (0, j))],
        out_specs=(tile,) * 6,
        scratch_shapes=[pltpu.VMEM((d, N_IN_SEGMENTS * tn), BF16), pltpu.VMEM((seq + 8, tn), F32)],
        compiler_params=_params("arbitrary", "arbitrary"),
        name="in_proj",
    )(hm, *([w_in] * N_IN_SEGMENTS), b_gate.reshape(1, 2 * d), b_gate.reshape(1, 2 * d), conv_w)


def _moba_kernel(kaug_ref, q_ref, k_ref, v_ref, o_ref, qa_ref, ka_ref, va_ref):
    seq, hd = q_ref.shape
    blk = MOBA_BLOCK
    nb = seq // blk

    q = q_ref[...]
    k = k_ref[...]

    kmean = jnp.mean(k.astype(F32).reshape(nb, blk, hd), axis=1)
    kmean_hi = kmean.astype(BF16)
    kmean_lo = (kmean - kmean_hi.astype(F32)).astype(BF16)
    gate2 = _dot_nt(jnp.concatenate([kmean_hi, kmean_lo], axis=0), q)
    gate = gate2[:nb] + gate2[nb:]
    blk_id = lax.broadcasted_iota(jnp.int32, (nb, seq), 0)
    q_blk = lax.broadcasted_iota(jnp.int32, (nb, seq), 1) // blk
    gate = jnp.where(blk_id < q_blk, gate, -jnp.inf)
    mask_rows = []
    for n in range(nb):
        g_n = gate[n:n + 1, :]
        beats = jnp.where(gate > g_n, 1.0, jnp.where((gate == g_n) & (blk_id < n), 1.0, 0.0))
        rank = jnp.sum(beats, axis=0, keepdims=True)
        q_blk_row = q_blk[0:1, :]
        visible = ((rank < float(MOBA_TOPK)) & (n < q_blk_row)) | (n == q_blk_row)
        mask_rows.append(jnp.where(visible, 0.0, MASK_NEG))
    mask_t = jnp.concatenate(mask_rows + [jnp.ones((ALIBI_PARTS, seq), F32),
                                          jnp.zeros((hd - nb - ALIBI_PARTS, seq), F32)], axis=0)
    mask_cols = mask_t.T.astype(BF16)

    qa_ref[:, :hd] = q
    qa_ref[:, hd:] = mask_cols
    ka_ref[:, :hd] = k
    ka_ref[:, hd:] = kaug_ref[...]
    va_ref[:, :hd] = v_ref[...]
    va_ref[:, hd:] = jnp.ones((seq, hd), BF16)

    causal = (lax.broadcasted_iota(jnp.int32, (blk, blk), 0)
              >= lax.broadcasted_iota(jnp.int32, (blk, blk), 1))

    def scores(qb):
        return _dot_nt(qa_ref[qb * blk:(qb + 1) * blk, :], ka_ref[0:(qb + 1) * blk, :])

    order = list(range(nb - 1, -1, -1))
    s_next = scores(order[0])
    for idx, qb in enumerate(order):
        s = s_next
        if idx + 1 < nb:
            s_next = scores(order[idx + 1])
        s_own = jnp.where(causal, s[:, qb * blk:], -jnp.inf)
        s = s_own if qb == 0 else jnp.concatenate([s[:, :qb * blk], s_own], axis=1)
        m = jnp.max(s, axis=-1, keepdims=True)
        p = jnp.exp2(s - m).astype(BF16)
        acc = _dot(p, va_ref[0:(qb + 1) * blk, :])
        o_ref[qb * blk:(qb + 1) * blk, :] = (acc[:, :hd] / acc[:, hd:]).astype(o_ref.dtype)


def _key_augmentation(seq):
    nb = seq // MOBA_BLOCK
    slopes = jnp.exp2(-8.0 * jnp.arange(1, N_HEADS + 1, dtype=F32) / N_HEADS)
    t_k = jnp.arange(seq, dtype=jnp.int32)[None, :, None]
    lane = lax.broadcasted_iota(jnp.int32, (1, 1, HEAD_DIM), 2)
    table = jnp.where(lane == t_k // MOBA_BLOCK, 1.0, 0.0)
    rest = LOG2_E * slopes[:, None, None] * t_k.astype(F32)
    for part in range(ALIBI_PARTS):
        top = lax.bitcast_convert_type(
            lax.bitcast_convert_type(rest, jnp.uint32) & jnp.uint32(0xFFFF0000), F32)
        table = jnp.where(lane == nb + part, top, table)
        rest = rest - top
    return table.astype(BF16)


def _moba(q, k, v, *, bsz, seq):
    t, d = q.shape
    tile = pl.BlockSpec((seq, HEAD_DIM), lambda b, h: (b, h))
    return pl.pallas_call(
        _moba_kernel,
        out_shape=jax.ShapeDtypeStruct((t, d), BF16),
        grid=(bsz, N_HEADS),
        in_specs=[pl.BlockSpec((None, seq, HEAD_DIM), lambda b, h: (h, 0, 0)), tile, tile, tile],
        out_specs=tile,
        scratch_shapes=[
            pltpu.VMEM((seq, 2 * HEAD_DIM), BF16),
            pltpu.VMEM((seq, 2 * HEAD_DIM), BF16),
            pltpu.VMEM((seq, 2 * HEAD_DIM), BF16),
        ],
        compiler_params=_params("parallel", "arbitrary"),
        name="moba",
    )(_key_augmentation(seq), q, k, v)


def _mix_out_kernel(a_ref, c_ref, wa_ref, wc_ref, ga_ref, gc_ref, wo_ref, x_ref, o_ref, *, row_chunk):
    @pl.when(pl.program_id(1) == 0)
    def _():
        o_ref[...] = x_ref[...]

    for r0 in range(0, o_ref.shape[0], row_chunk):
        rows = pl.ds(r0, row_chunk)
        ya = _dot(a_ref[rows, :], wa_ref[...])
        yc = _dot(c_ref[rows, :], wc_ref[...])
        m = (ga_ref[rows, :].astype(F32) * ya + gc_ref[rows, :].astype(F32) * yc).astype(BF16)
        o_ref[rows, :] += _dot(m, wo_ref[...])


def _mix_out(attn, conv, wa, wc, ga, gc, wo, x1):
    t, d = attn.shape
    tm, tn = MIX_TOKEN_TILE, MIX_CHANNEL_TILE
    row = pl.BlockSpec((tm, d), lambda i, j: (i, 0))
    col = pl.BlockSpec((d, tn), lambda i, j: (0, j))
    tile = pl.BlockSpec((tm, tn), lambda i, j: (i, j))
    return pl.pallas_call(
        functools.partial(_mix_out_kernel, row_chunk=MIX_ROW_CHUNK),
        out_shape=jax.ShapeDtypeStruct((t, d), F32),
        grid=(t // tm, d // tn),
        in_specs=[row, row, col, col, tile, tile, pl.BlockSpec((tn, d), lambda i, j: (j, 0)), row],
        out_specs=row,
        compiler_params=_params("parallel", "arbitrary"),
        name="mix_out",
    )(attn, conv, wa, wc, ga, gc, wo, x1)


def kernel(x, ffn1_norm, ffn1_w_gate, ffn1_w_up, ffn1_w_down, mix_norm, w_in, b_gate, conv_w,
           w_attn_out, w_conv_out, w_out, ffn2_norm, ffn2_w_gate, ffn2_w_up, ffn2_w_down, final_norm):
    bsz, seq, d = x.shape
    depth = ffn1_norm.shape[0]
    assert d == N_HEADS * HEAD_DIM and seq % MOBA_BLOCK == 0 and seq // MOBA_BLOCK <= HEAD_DIM
    xs = x.reshape(bsz * seq, d)
    for l in range(depth):
        last = l == depth - 1
        x1, hm = _ffn(xs, ffn1_norm[l], ffn1_w_gate[l].astype(BF16), ffn1_w_up[l].astype(BF16),
                      ffn1_w_down[l].astype(BF16), mix_norm[l],
                      emit_residual=True, name="ffn1")
        q, k, v, conv, ga, gc = _in_proj(hm, w_in[l], b_gate[l], conv_w[l], bsz=bsz, seq=seq)
        attn = _moba(q, k, v, bsz=bsz, seq=seq)
        x2 = _mix_out(attn, conv, w_attn_out[l].astype(BF16), w_conv_out[l].astype(BF16), ga, gc,
                      w_out[l].astype(BF16), x1)
        if last:
            xs = _ffn(x2, ffn2_norm[l], ffn2_w_gate[l].astype(BF16), ffn2_w_up[l].astype(BF16),
                      ffn2_w_down[l].astype(BF16), final_norm,
                      emit_residual=False, name="ffn2")
        else:
            one = jnp.ones((d,), F32)
            xs, _ = _ffn(x2, ffn2_norm[l], ffn2_w_gate[l].astype(BF16), ffn2_w_up[l].astype(BF16),
                         ffn2_w_down[l].astype(BF16), one,
                         emit_residual=True, name="ffn2")
    return xs.reshape(bsz, seq, d)
```

```python
import functools

import jax
import jax.numpy as jnp
from jax import lax
from jax.experimental import pallas as pl
from jax.experimental.pallas import tpu as pltpu

F32 = jnp.float32
BF16 = jnp.bfloat16

N_HEADS = 16
HEAD_DIM = 128
CONV_KERNEL = 3
MOBA_BLOCK = 256
MOBA_TOPK = 3
ALIBI_PARTS = 3
RMS_EPS = 1e-6
N_IN_SEGMENTS = 8

V7X_VMEM_BYTES = 64 * 1024 * 1024
VMEM_LIMIT_BYTES = V7X_VMEM_BYTES - 8 * 1024 * 1024

FFN_TOKEN_TILE = 1024
FFN_DFF_TILE = 512
FFN_ROW_CHUNK = 512

IN_PROJ_CHANNEL_TILE = 128
IN_PROJ_ROW_CHUNK = 512

MIX_TOKEN_TILE = 256
MIX_CHANNEL_CHUNK = 512

LOG2_E = 1.4426950408889634
QK_LOG2_SCALE = LOG2_E * HEAD_DIM ** -0.5

MASK_NEG = -1e30


def _params(*semantics):
    return pltpu.CompilerParams(dimension_semantics=semantics, vmem_limit_bytes=VMEM_LIMIT_BYTES)


def _rms_norm(x, g):
    return x * lax.rsqrt(jnp.mean(x * x, axis=-1, keepdims=True) + RMS_EPS) * g


def _dot(a, b):
    return jnp.dot(a, b, preferred_element_type=F32)


def _dot_nt(a, b):
    return lax.dot_general(a, b, (((1,), (1,)), ((), ())), preferred_element_type=F32)


def _ffn_kernel(x_hbm, gin_ref, wg_ref, wu_ref, wd_ref, gout_ref, *rest, emit_residual, row_chunk):
    if emit_residual:
        y_hbm, n_hbm, h_ref, acc_ref, x_buf, y_buf, n_buf, sems = rest
    else:
        n_hbm, h_ref, acc_ref, x_buf, n_buf, sems = rest
        y_hbm = y_buf = None
    tm = x_buf.shape[0]
    i = pl.program_id(0)
    f = pl.program_id(1)
    n_tiles = pl.num_programs(0)

    def tile(ref, t):
        return ref.at[pl.ds(pl.multiple_of(t * tm, tm), tm), :]

    def x_copy(t):
        return pltpu.make_async_copy(tile(x_hbm, t), x_buf, sems.at[0])

    def y_copy(t):
        return pltpu.make_async_copy(y_buf, tile(y_hbm, t), sems.at[1])

    def n_copy(t):
        return pltpu.make_async_copy(n_buf, tile(n_hbm, t), sems.at[2])

    @pl.when(f == 0)
    def _():
        @pl.when(i == 0)
        def _():
            x_copy(0).start()

        x_copy(i).wait()
        x = x_buf[...]
        h_ref[...] = _rms_norm(x, gin_ref[...]).astype(BF16)
        acc_ref[...] = x

        @pl.when(i + 1 < n_tiles)
        def _():
            x_copy(i + 1).start()

    for r0 in range(0, tm, row_chunk):
        rows = pl.ds(r0, row_chunk)
        h = h_ref[rows, :]
        g = _dot(h, wg_ref[...])
        u = _dot(h, wu_ref[...])
        a = (0.5 * g * jax.nn.sigmoid(g) * u).astype(BF16)
        acc_ref[rows, :] += _dot(a, wd_ref[...])

    @pl.when(f == pl.num_programs(1) - 1)
    def _():
        @pl.when(i > 0)
        def _():
            if emit_residual:
                y_copy(i - 1).wait()
            n_copy(i - 1).wait()

        y = acc_ref[...]
        if emit_residual:
            y_buf[...] = y
            y_copy(i).start()
        n_buf[...] = _rms_norm(y, gout_ref[...]).astype(n_buf.dtype)
        n_copy(i).start()

        @pl.when(i == n_tiles - 1)
        def _():
            if emit_residual:
                y_copy(i).wait()
            n_copy(i).wait()


def _ffn(x, g_in, wg, wu, wd, g_out, *, emit_residual, name):
    t, d = x.shape
    d_ff = wg.shape[1]
    tm, tf = FFN_TOKEN_TILE, FFN_DFF_TILE
    hbm = pl.BlockSpec(memory_space=pl.ANY)
    vec = pl.BlockSpec((1, d), lambda i, f: (0, 0))
    n_dtype = BF16 if emit_residual else F32
    n_shape = jax.ShapeDtypeStruct((t, d), n_dtype)
    if emit_residual:
        out_shape = (jax.ShapeDtypeStruct((t, d), F32), n_shape)
        out_specs = (hbm, hbm)
        out_bufs = [pltpu.VMEM((tm, d), F32), pltpu.VMEM((tm, d), n_dtype)]
    else:
        out_shape = n_shape
        out_specs = hbm
        out_bufs = [pltpu.VMEM((tm, d), n_dtype)]
    return pl.pallas_call(
        functools.partial(_ffn_kernel, emit_residual=emit_residual, row_chunk=FFN_ROW_CHUNK),
        out_shape=out_shape,
        grid=(t // tm, d_ff // tf),
        in_specs=[
            hbm, vec,
            pl.BlockSpec((d, tf), lambda i, f: (0, f)),
            pl.BlockSpec((d, tf), lambda i, f: (0, f)),
            pl.BlockSpec((tf, d), lambda i, f: (f, 0)),
            vec,
        ],
        out_specs=out_specs,
        scratch_shapes=[pltpu.VMEM((tm, d), BF16), pltpu.VMEM((tm, d), F32), pltpu.VMEM((tm, d), F32)]
        + out_bufs + [pltpu.SemaphoreType.DMA((3,))],
        compiler_params=_params("arbitrary", "arbitrary"),
        name=name,
    )(x, g_in.reshape(1, d), wg, wu, wd, g_out.reshape(1, d))


def _in_proj_kernel(w_hbm, h_ref, bga_ref, bgc_ref, cw_ref, q_ref, k_ref, v_ref, cp_ref, ga_ref, gc_ref,
                    wstage_ref, wcat_ref, u_ref, sems, *, row_chunk):
    seq, tn = q_ref.shape
    d = h_ref.shape[1]
    j = pl.program_id(0)

    def w_copy(seg, tile_idx):
        col = pl.multiple_of(seg * d + tile_idx * tn, tn)
        return pltpu.make_async_copy(w_hbm.at[:, pl.ds(col, tn)], wstage_ref.at[seg], sems.at[seg])

    @pl.when(pl.program_id(1) == 0)
    def _():
        @pl.when(j == 0)
        def _():
            for seg in range(N_IN_SEGMENTS):
                w_copy(seg, 0).start()

        for seg in range(N_IN_SEGMENTS):
            w_copy(seg, j).wait()
            wcat_ref[:, seg * tn:(seg + 1) * tn] = wstage_ref[seg].astype(BF16)

        @pl.when(j + 1 < pl.num_programs(0))
        def _():
            for seg in range(N_IN_SEGMENTS):
                w_copy(seg, j + 1).start()

    halo = 8
    u_ref[0:halo, :] = jnp.zeros((halo, tn), F32)
    cw = cw_ref[...]
    for r0 in range(0, seq, row_chunk):
        rows = pl.ds(r0, row_chunk)
        z = _dot(h_ref[rows, :], wcat_ref[...])
        zq, zk, zv, zb, zc, zx, zga, zgc = [z[:, seg * tn:(seg + 1) * tn] for seg in range(N_IN_SEGMENTS)]
        q_ref[rows, :] = (zq * QK_LOG2_SCALE).astype(BF16)
        k_ref[rows, :] = zk.astype(BF16)
        v_ref[rows, :] = zv.astype(BF16)
        ga_ref[rows, :] = jax.nn.sigmoid(zga + bga_ref[...]).astype(BF16)
        gc_ref[rows, :] = jax.nn.sigmoid(zgc + bgc_ref[...]).astype(BF16)
        u = zc * zx
        u_ref[pl.ds(halo + r0, row_chunk), :] = u
        conv = cw[2:3, :] * u
        for back in range(1, CONV_KERNEL):
            conv += cw[2 - back:3 - back, :] * u_ref[pl.ds(halo + r0 - back, row_chunk), :]
        cp_ref[rows, :] = (zb * conv).astype(BF16)


def _in_proj(hm, w_in, b_gate, conv_w, *, bsz, seq):
    t, d = hm.shape
    tn = IN_PROJ_CHANNEL_TILE
    nj = d // tn
    assert w_in.shape == (d, N_IN_SEGMENTS * d) and t == bsz * seq

    def b_spec(seg):
        return pl.BlockSpec((1, tn), lambda j, b, seg=seg: (0, seg * nj + j))

    tile = pl.BlockSpec((seq, tn), lambda j, b: (b, j))
    out = jax.ShapeDtypeStruct((t, d), BF16)
    return pl.pallas_call(
        functools.partial(_in_proj_kernel, row_chunk=IN_PROJ_ROW_CHUNK),
        out_shape=(out,) * 6,
        grid=(nj, bsz),
        in_specs=[pl.BlockSpec(memory_space=pl.ANY), pl.BlockSpec((seq, d), lambda j, b: (b, 0)),
                  b_spec(0), b_spec(1), pl.BlockSpec((CONV_KERNEL, tn), lambda j, b: (0, j))],
        out_specs=(tile,) * 6,
        scratch_shapes=[pltpu.VMEM((N_IN_SEGMENTS, d, tn), F32),
                        pltpu.VMEM((d, N_IN_SEGMENTS * tn), BF16),
                        pltpu.VMEM((seq + 8, tn), F32),
                        pltpu.SemaphoreType.DMA((N_IN_SEGMENTS,))],
        compiler_params=_params("arbitrary", "arbitrary"),
        name="in_proj",
    )(w_in, hm, b_gate.reshape(1, 2 * d), b_gate.reshape(1, 2 * d), conv_w)


def _moba_kernel(kaug_ref, q_ref, k_ref, v_ref, o_ref, qa_ref, ka_ref, va_ref):
    seq, hd = q_ref.shape
    blk = MOBA_BLOCK
    nb = seq // blk

    q = q_ref[...]
    k = k_ref[...]

    kmean = jnp.mean(k.astype(F32).reshape(nb, blk, hd), axis=1)
    kmean_hi = kmean.astype(BF16)
    kmean_lo = (kmean - kmean_hi.astype(F32)).astype(BF16)
    gate2 = _dot_nt(jnp.concatenate([kmean_hi, kmean_lo], axis=0), q)
    gate = gate2[:nb] + gate2[nb:]
    blk_id = lax.broadcasted_iota(jnp.int32, (nb, seq), 0)
    q_blk = lax.broadcasted_iota(jnp.int32, (nb, seq), 1) // blk
    gate = jnp.where(blk_id < q_blk, gate, -jnp.inf)
    mask_rows = []
    for n in range(nb):
        g_n = gate[n:n + 1, :]
        beats = jnp.where(gate > g_n, 1.0, jnp.where((gate == g_n) & (blk_id < n), 1.0, 0.0))
        rank = jnp.sum(beats, axis=0, keepdims=True)
        q_blk_row = q_blk[0:1, :]
        visible = ((rank < float(MOBA_TOPK)) & (n < q_blk_row)) | (n == q_blk_row)
        mask_rows.append(jnp.where(visible, 0.0, MASK_NEG))
    mask_t = jnp.concatenate(mask_rows + [jnp.ones((ALIBI_PARTS, seq), F32),
                                          jnp.zeros((hd - nb - ALIBI_PARTS, seq), F32)], axis=0)
    mask_cols = mask_t.T.astype(BF16)

    qa_ref[:, :hd] = q
    qa_ref[:, hd:] = mask_cols
    ka_ref[:, :hd] = k
    ka_ref[:, hd:] = kaug_ref[...]
    va_ref[:, :hd] = v_ref[...]
    va_ref[:, hd:] = jnp.ones((seq, hd), BF16)

    causal = (lax.broadcasted_iota(jnp.int32, (blk, blk), 0)
              >= lax.broadcasted_iota(jnp.int32, (blk, blk), 1))

    def scores(qb):
        return _dot_nt(qa_ref[qb * blk:(qb + 1) * blk, :], ka_ref[0:(qb + 1) * blk, :])

    order = list(range(nb - 1, -1, -1))
    s_next = scores(order[0])
    for idx, qb in enumerate(order):
        s = s_next
        if idx + 1 < nb:
            s_next = scores(order[idx + 1])
        s_own = jnp.where(causal, s[:, qb * blk:], -jnp.inf)
        s = s_own if qb == 0 else jnp.concatenate([s[:, :qb * blk], s_own], axis=1)
        m = jnp.max(s, axis=-1, keepdims=True)
        p = jnp.exp2(s - m).astype(BF16)
        acc = _dot(p, va_ref[0:(qb + 1) * blk, :])
        o_ref[qb * blk:(qb + 1) * blk, :] = (acc[:, :hd] / acc[:, hd:]).astype(o_ref.dtype)


def _key_augmentation(seq):
    nb = seq // MOBA_BLOCK
    slopes = jnp.exp2(-8.0 * jnp.arange(1, N_HEADS + 1, dtype=F32) / N_HEADS)
    t_k = jnp.arange(seq, dtype=jnp.int32)[None, :, None]
    lane = lax.broadcasted_iota(jnp.int32, (1, 1, HEAD_DIM), 2)
    table = jnp.where(lane == t_k // MOBA_BLOCK, 1.0, 0.0)
    rest = LOG2_E * slopes[:, None, None] * t_k.astype(F32)
    for part in range(ALIBI_PARTS):
        top = lax.bitcast_convert_type(
            lax.bitcast_convert_type(rest, jnp.uint32) & jnp.uint32(0xFFFF0000), F32)
        table = jnp.where(lane == nb + part, top, table)
        rest = rest - top
    return table.astype(BF16)


def _moba(q, k, v, *, bsz, seq):
    t, d = q.shape
    tile = pl.BlockSpec((seq, HEAD_DIM), lambda b, h: (b, h))
    return pl.pallas_call(
        _moba_kernel,
        out_shape=jax.ShapeDtypeStruct((t, d), BF16),
        grid=(bsz, N_HEADS),
        in_specs=[pl.BlockSpec((None, seq, HEAD_DIM), lambda b, h: (h, 0, 0)), tile, tile, tile],
        out_specs=tile,
        scratch_shapes=[
            pltpu.VMEM((seq, 2 * HEAD_DIM), BF16),
            pltpu.VMEM((seq, 2 * HEAD_DIM), BF16),
            pltpu.VMEM((seq, 2 * HEAD_DIM), BF16),
        ],
        compiler_params=_params("parallel", "arbitrary"),
        name="moba",
    )(_key_augmentation(seq), q, k, v)


def _mix_out_kernel(a_ref, c_ref, ga_ref, gc_ref, x_ref, wa_ref, wc_ref, wo_ref, o_ref, *, chunk):
    a = a_ref[...]
    c = c_ref[...]
    acc = x_ref[...]
    for c0 in range(0, wa_ref.shape[1], chunk):
        cols = pl.ds(c0, chunk)
        ya = _dot(a, wa_ref[:, cols])
        yc = _dot(c, wc_ref[:, cols])
        m = (ga_ref[:, cols].astype(F32) * ya + gc_ref[:, cols].astype(F32) * yc).astype(BF16)
        acc = acc + _dot(m, wo_ref[cols, :])
    o_ref[...] = acc


def _mix_out(attn, conv, wa, wc, ga, gc, wo, x1):
    t, d = attn.shape
    tm = MIX_TOKEN_TILE
    row = pl.BlockSpec((tm, d), lambda i: (i, 0))
    resident = pl.BlockSpec((d, d), lambda i: (0, 0), pipeline_mode=pl.Buffered(1))
    return pl.pallas_call(
        functools.partial(_mix_out_kernel, chunk=MIX_CHANNEL_CHUNK),
        out_shape=jax.ShapeDtypeStruct((t, d), F32),
        grid=(t // tm,),
        in_specs=[row, row, row, row, row, resident, resident, resident],
        out_specs=row,
        compiler_params=_params("parallel"),
        name="mix_out",
    )(attn, conv, ga, gc, x1, wa, wc, wo)


def kernel(x, ffn1_norm, ffn1_w_gate, ffn1_w_up, ffn1_w_down, mix_norm, w_in, b_gate, conv_w,
           w_attn_out, w_conv_out, w_out, ffn2_norm, ffn2_w_gate, ffn2_w_up, ffn2_w_down, final_norm):
    bsz, seq, d = x.shape
    depth = ffn1_norm.shape[0]
    assert d == N_HEADS * HEAD_DIM and seq % MOBA_BLOCK == 0 and seq // MOBA_BLOCK <= HEAD_DIM
    xs = x.reshape(bsz * seq, d)
    for l in range(depth):
        last = l == depth - 1
        x1, hm = _ffn(xs, ffn1_norm[l], ffn1_w_gate[l].astype(BF16), ffn1_w_up[l].astype(BF16),
                      ffn1_w_down[l].astype(BF16), mix_norm[l],
                      emit_residual=True, name="ffn1")
        q, k, v, conv, ga, gc = _in_proj(hm, w_in[l], b_gate[l], conv_w[l], bsz=bsz, seq=seq)
        attn = _moba(q, k, v, bsz=bsz, seq=seq)
        x2 = _mix_out(attn, conv, w_attn_out[l].astype(BF16), w_conv_out[l].astype(BF16), ga, gc,
                      w_out[l].astype(BF16), x1)
        if last:
            xs = _ffn(x2, ffn2_norm[l], ffn2_w_gate[l].astype(BF16), ffn2_w_up[l].astype(BF16),
                      ffn2_w_down[l].astype(BF16), final_norm,
                      emit_residual=False, name="ffn2")
        else:
            one = jnp.ones((d,), F32)
            xs, _ = _ffn(x2, ffn2_norm[l], ffn2_w_gate[l].astype(BF16), ffn2_w_up[l].astype(BF16),
                         ffn2_w_down[l].astype(BF16), one,
                         emit_residual=True, name="ffn2")
    return xs.reshape(bsz, seq, d)
```

```python
import functools

import jax
import jax.numpy as jnp
from jax import lax
from jax.experimental import pallas as pl
from jax.experimental.pallas import tpu as pltpu

F32 = jnp.float32
BF16 = jnp.bfloat16

N_HEADS = 16
HEAD_DIM = 128
CONV_KERNEL = 3
MOBA_BLOCK = 256
MOBA_TOPK = 3
ALIBI_PARTS = 3
RMS_EPS = 1e-6
N_IN_SEGMENTS = 8

V7X_VMEM_BYTES = 64 * 1024 * 1024
VMEM_LIMIT_BYTES = V7X_VMEM_BYTES - 8 * 1024 * 1024

FFN_TOKEN_TILE = 1024
FFN_DFF_TILE = 256
FFN_ROW_CHUNK = 512

IN_PROJ_CHANNEL_TILE = 128
IN_PROJ_ROW_CHUNK = 512

MIX_TOKEN_TILE = 256
MIX_CHANNEL_CHUNK = 512

LOG2_E = 1.4426950408889634
QK_LOG2_SCALE = LOG2_E * HEAD_DIM ** -0.5

MASK_NEG = -1e30


def _params(*semantics):
    return pltpu.CompilerParams(dimension_semantics=semantics, vmem_limit_bytes=VMEM_LIMIT_BYTES)


def _rms_norm(x, g):
    return x * lax.rsqrt(jnp.mean(x * x, axis=-1, keepdims=True) + RMS_EPS) * g


def _dot(a, b):
    return jnp.dot(a, b, preferred_element_type=F32)


def _dot_nt(a, b):
    return lax.dot_general(a, b, (((1,), (1,)), ((), ())), preferred_element_type=F32)


def _ffn_kernel(x_hbm, gin_ref, wg_ref, wu_ref, wd_ref, gout_ref, *rest, emit_residual, row_chunk):
    if emit_residual:
        y_hbm, n_hbm, h_ref, acc_ref, x_buf, y_buf, n_buf, sems = rest
    else:
        n_hbm, h_ref, acc_ref, x_buf, n_buf, sems = rest
        y_hbm = y_buf = None
    tm = x_buf.shape[0]
    i = pl.program_id(0)
    f = pl.program_id(1)
    n_tiles = pl.num_programs(0)

    def tile(ref, t):
        return ref.at[pl.ds(pl.multiple_of(t * tm, tm), tm), :]

    def x_copy(t):
        return pltpu.make_async_copy(tile(x_hbm, t), x_buf, sems.at[0])

    def y_copy(t):
        return pltpu.make_async_copy(y_buf, tile(y_hbm, t), sems.at[1])

    def n_copy(t):
        return pltpu.make_async_copy(n_buf, tile(n_hbm, t), sems.at[2])

    @pl.when(f == 0)
    def _():
        @pl.when(i == 0)
        def _():
            x_copy(0).start()

        x_copy(i).wait()
        x = x_buf[...]
        h_ref[...] = _rms_norm(x, gin_ref[...]).astype(BF16)
        acc_ref[...] = x

        @pl.when(i + 1 < n_tiles)
        def _():
            x_copy(i + 1).start()

    wg = wg_ref[...].astype(BF16)
    wu = wu_ref[...].astype(BF16)
    wd = wd_ref[...].astype(BF16)
    for r0 in range(0, tm, row_chunk):
        rows = pl.ds(r0, row_chunk)
        h = h_ref[rows, :]
        g = _dot(h, wg)
        u = _dot(h, wu)
        a = (0.5 * g * jax.nn.sigmoid(g) * u).astype(BF16)
        acc_ref[rows, :] += _dot(a, wd)

    @pl.when(f == pl.num_programs(1) - 1)
    def _():
        @pl.when(i > 0)
        def _():
            if emit_residual:
                y_copy(i - 1).wait()
            n_copy(i - 1).wait()

        y = acc_ref[...]
        if emit_residual:
            y_buf[...] = y
            y_copy(i).start()
        n_buf[...] = _rms_norm(y, gout_ref[...]).astype(n_buf.dtype)
        n_copy(i).start()

        @pl.when(i == n_tiles - 1)
        def _():
            if emit_residual:
                y_copy(i).wait()
            n_copy(i).wait()


def _ffn(x, g_in, wg, wu, wd, g_out, *, emit_residual, name):
    t, d = x.shape
    d_ff = wg.shape[1]
    tm, tf = FFN_TOKEN_TILE, FFN_DFF_TILE
    hbm = pl.BlockSpec(memory_space=pl.ANY)
    vec = pl.BlockSpec((1, d), lambda i, f: (0, 0))
    n_dtype = BF16 if emit_residual else F32
    n_shape = jax.ShapeDtypeStruct((t, d), n_dtype)
    if emit_residual:
        out_shape = (jax.ShapeDtypeStruct((t, d), F32), n_shape)
        out_specs = (hbm, hbm)
        out_bufs = [pltpu.VMEM((tm, d), F32), pltpu.VMEM((tm, d), n_dtype)]
    else:
        out_shape = n_shape
        out_specs = hbm
        out_bufs = [pltpu.VMEM((tm, d), n_dtype)]
    return pl.pallas_call(
        functools.partial(_ffn_kernel, emit_residual=emit_residual, row_chunk=FFN_ROW_CHUNK),
        out_shape=out_shape,
        grid=(t // tm, d_ff // tf),
        in_specs=[
            hbm, vec,
            pl.BlockSpec((d, tf), lambda i, f: (0, f)),
            pl.BlockSpec((d, tf), lambda i, f: (0, f)),
            pl.BlockSpec((tf, d), lambda i, f: (f, 0)),
            vec,
        ],
        out_specs=out_specs,
        scratch_shapes=[pltpu.VMEM((tm, d), BF16), pltpu.VMEM((tm, d), F32), pltpu.VMEM((tm, d), F32)]
        + out_bufs + [pltpu.SemaphoreType.DMA((3,))],
        compiler_params=_params("arbitrary", "arbitrary"),
        name=name,
    )(x, g_in.reshape(1, d), wg, wu, wd, g_out.reshape(1, d))


def _in_proj_kernel(w_hbm, h_ref, bga_ref, bgc_ref, cw_ref, q_ref, k_ref, v_ref, cp_ref, ga_ref, gc_ref,
                    wstage_ref, wcat_ref, u_ref, sems, *, row_chunk):
    seq, tn = q_ref.shape
    d = h_ref.shape[1]
    j = pl.program_id(0)

    def w_copy(seg, tile_idx):
        col = pl.multiple_of(seg * d + tile_idx * tn, tn)
        return pltpu.make_async_copy(w_hbm.at[:, pl.ds(col, tn)], wstage_ref.at[seg], sems.at[seg])

    @pl.when(pl.program_id(1) == 0)
    def _():
        @pl.when(j == 0)
        def _():
            for seg in range(N_IN_SEGMENTS):
                w_copy(seg, 0).start()

        for seg in range(N_IN_SEGMENTS):
            w_copy(seg, j).wait()
            wcat_ref[:, seg * tn:(seg + 1) * tn] = wstage_ref[seg].astype(BF16)

        @pl.when(j + 1 < pl.num_programs(0))
        def _():
            for seg in range(N_IN_SEGMENTS):
                w_copy(seg, j + 1).start()

    halo = 8
    u_ref[0:halo, :] = jnp.zeros((halo, tn), F32)
    cw = cw_ref[...]
    for r0 in range(0, seq, row_chunk):
        rows = pl.ds(r0, row_chunk)
        z = _dot(h_ref[rows, :], wcat_ref[...])
        zq, zk, zv, zb, zc, zx, zga, zgc = [z[:, seg * tn:(seg + 1) * tn] for seg in range(N_IN_SEGMENTS)]
        q_ref[rows, :] = (zq * QK_LOG2_SCALE).astype(BF16)
        k_ref[rows, :] = zk.astype(BF16)
        v_ref[rows, :] = zv.astype(BF16)
        ga_ref[rows, :] = jax.nn.sigmoid(zga + bga_ref[...]).astype(BF16)
        gc_ref[rows, :] = jax.nn.sigmoid(zgc + bgc_ref[...]).astype(BF16)
        u = zc * zx
        u_ref[pl.ds(halo + r0, row_chunk), :] = u
        conv = cw[2:3, :] * u
        for back in range(1, CONV_KERNEL):
            conv += cw[2 - back:3 - back, :] * u_ref[pl.ds(halo + r0 - back, row_chunk), :]
        cp_ref[rows, :] = (zb * conv).astype(BF16)


def _in_proj(hm, w_in, b_gate, conv_w, *, bsz, seq):
    t, d = hm.shape
    tn = IN_PROJ_CHANNEL_TILE
    nj = d // tn
    assert w_in.shape == (d, N_IN_SEGMENTS * d) and t == bsz * seq

    def b_spec(seg):
        return pl.BlockSpec((1, tn), lambda j, b, seg=seg: (0, seg * nj + j))

    tile = pl.BlockSpec((seq, tn), lambda j, b: (b, j))
    out = jax.ShapeDtypeStruct((t, d), BF16)
    return pl.pallas_call(
        functools.partial(_in_proj_kernel, row_chunk=IN_PROJ_ROW_CHUNK),
        out_shape=(out,) * 6,
        grid=(nj, bsz),
        in_specs=[pl.BlockSpec(memory_space=pl.ANY), pl.BlockSpec((seq, d), lambda j, b: (b, 0)),
                  b_spec(0), b_spec(1), pl.BlockSpec((CONV_KERNEL, tn), lambda j, b: (0, j))],
        out_specs=(tile,) * 6,
        scratch_shapes=[pltpu.VMEM((N_IN_SEGMENTS, d, tn), F32),
                        pltpu.VMEM((d, N_IN_SEGMENTS * tn), BF16),
                        pltpu.VMEM((seq + 8, tn), F32),
                        pltpu.SemaphoreType.DMA((N_IN_SEGMENTS,))],
        compiler_params=_params("arbitrary", "arbitrary"),
        name="in_proj",
    )(w_in, hm, b_gate.reshape(1, 2 * d), b_gate.reshape(1, 2 * d), conv_w)


def _moba_kernel(kaug_ref, q_ref, k_ref, v_ref, o_ref, qa_ref, ka_ref, va_ref):
    seq, hd = q_ref.shape
    blk = MOBA_BLOCK
    nb = seq // blk

    q = q_ref[...]
    k = k_ref[...]

    kmean = jnp.mean(k.astype(F32).reshape(nb, blk, hd), axis=1)
    kmean_hi = kmean.astype(BF16)
    kmean_lo = (kmean - kmean_hi.astype(F32)).astype(BF16)
    gate2 = _dot_nt(jnp.concatenate([kmean_hi, kmean_lo], axis=0), q)
    gate = gate2[:nb] + gate2[nb:]
    blk_id = lax.broadcasted_iota(jnp.int32, (nb, seq), 0)
    q_blk = lax.broadcasted_iota(jnp.int32, (nb, seq), 1) // blk
    gate = jnp.where(blk_id < q_blk, gate, -jnp.inf)
    mask_rows = []
    for n in range(nb):
        g_n = gate[n:n + 1, :]
        beats = jnp.where(gate > g_n, 1.0, jnp.where((gate == g_n) & (blk_id < n), 1.0, 0.0))
        rank = jnp.sum(beats, axis=0, keepdims=True)
        q_blk_row = q_blk[0:1, :]
        visible = ((rank < float(MOBA_TOPK)) & (n < q_blk_row)) | (n == q_blk_row)
        mask_rows.append(jnp.where(visible, 0.0, MASK_NEG))
    mask_t = jnp.concatenate(mask_rows + [jnp.ones((ALIBI_PARTS, seq), F32),
                                          jnp.zeros((hd - nb - ALIBI_PARTS, seq), F32)], axis=0)
    mask_cols = mask_t.T.astype(BF16)

    qa_ref[:, :hd] = q
    qa_ref[:, hd:] = mask_cols
    ka_ref[:, :hd] = k
    ka_ref[:, hd:] = kaug_ref[...]
    va_ref[:, :hd] = v_ref[...]
    va_ref[:, hd:] = jnp.ones((seq, hd), BF16)

    causal = (lax.broadcasted_iota(jnp.int32, (blk, blk), 0)
              >= lax.broadcasted_iota(jnp.int32, (blk, blk), 1))

    def scores(qb):
        return _dot_nt(qa_ref[qb * blk:(qb + 1) * blk, :], ka_ref[0:(qb + 1) * blk, :])

    order = list(range(nb - 1, -1, -1))
    s_next = scores(order[0])
    for idx, qb in enumerate(order):
        s = s_next
        if idx + 1 < nb:
            s_next = scores(order[idx + 1])
        s_own = jnp.where(causal, s[:, qb * blk:], -jnp.inf)
        s = s_own if qb == 0 else jnp.concatenate([s[:, :qb * blk], s_own], axis=1)
        m = jnp.max(s, axis=-1, keepdims=True)
        p = jnp.exp2(s - m).astype(BF16)
        acc = _dot(p, va_ref[0:(qb + 1) * blk, :])
        o_ref[qb * blk:(qb + 1) * blk, :] = (acc[:, :hd] / acc[:, hd:]).astype(o_ref.dtype)


def _key_augmentation(seq):
    nb = seq // MOBA_BLOCK
    slopes = jnp.exp2(-8.0 * jnp.arange(1, N_HEADS + 1, dtype=F32) / N_HEADS)
    t_k = jnp.arange(seq, dtype=jnp.int32)[None, :, None]
    lane = lax.broadcasted_iota(jnp.int32, (1, 1, HEAD_DIM), 2)
    table = jnp.where(lane == t_k // MOBA_BLOCK, 1.0, 0.0)
    rest = LOG2_E * slopes[:, None, None] * t_k.astype(F32)
    for part in range(ALIBI_PARTS):
        top = lax.bitcast_convert_type(
            lax.bitcast_convert_type(rest, jnp.uint32) & jnp.uint32(0xFFFF0000), F32)
        table = jnp.where(lane == nb + part, top, table)
        rest = rest - top
    return table.astype(BF16)


def _moba(q, k, v, *, bsz, seq):
    t, d = q.shape
    tile = pl.BlockSpec((seq, HEAD_DIM), lambda b, h: (b, h))
    return pl.pallas_call(
        _moba_kernel,
        out_shape=jax.ShapeDtypeStruct((t, d), BF16),
        grid=(bsz, N_HEADS),
        in_specs=[pl.BlockSpec((None, seq, HEAD_DIM), lambda b, h: (h, 0, 0)), tile, tile, tile],
        out_specs=tile,
        scratch_shapes=[
            pltpu.VMEM((seq, 2 * HEAD_DIM), BF16),
            pltpu.VMEM((seq, 2 * HEAD_DIM), BF16),
            pltpu.VMEM((seq, 2 * HEAD_DIM), BF16),
        ],
        compiler_params=_params("parallel", "arbitrary"),
        name="moba",
    )(_key_augmentation(seq), q, k, v)


def _mix_out_kernel(a_ref, c_ref, ga_ref, gc_ref, x_ref, wa_ref, wc_ref, wo_ref, o_ref, *, chunk):
    a = a_ref[...]
    c = c_ref[...]
    acc = x_ref[...]
    for c0 in range(0, wa_ref.shape[1], chunk):
        cols = pl.ds(c0, chunk)
        ya = _dot(a, wa_ref[:, cols])
        yc = _dot(c, wc_ref[:, cols])
        m = (ga_ref[:, cols].astype(F32) * ya + gc_ref[:, cols].astype(F32) * yc).astype(BF16)
        acc = acc + _dot(m, wo_ref[cols, :])
    o_ref[...] = acc


def _mix_out(attn, conv, wa, wc, ga, gc, wo, x1):
    t, d = attn.shape
    tm = MIX_TOKEN_TILE
    row = pl.BlockSpec((tm, d), lambda i: (i, 0))
    resident = pl.BlockSpec((d, d), lambda i: (0, 0), pipeline_mode=pl.Buffered(1))
    return pl.pallas_call(
        functools.partial(_mix_out_kernel, chunk=MIX_CHANNEL_CHUNK),
        out_shape=jax.ShapeDtypeStruct((t, d), F32),
        grid=(t // tm,),
        in_specs=[row, row, row, row, row, resident, resident, resident],
        out_specs=row,
        compiler_params=_params("parallel"),
        name="mix_out",
    )(attn, conv, ga, gc, x1, wa, wc, wo)


def kernel(x, ffn1_norm, ffn1_w_gate, ffn1_w_up, ffn1_w_down, mix_norm, w_in, b_gate, conv_w,
           w_attn_out, w_conv_out, w_out, ffn2_norm, ffn2_w_gate, ffn2_w_up, ffn2_w_down, final_norm):
    bsz, seq, d = x.shape
    depth = ffn1_norm.shape[0]
    assert d == N_HEADS * HEAD_DIM and seq % MOBA_BLOCK == 0 and seq // MOBA_BLOCK <= HEAD_DIM
    xs = x.reshape(bsz * seq, d)
    for l in range(depth):
        last = l == depth - 1
        x1, hm = _ffn(xs, ffn1_norm[l], ffn1_w_gate[l], ffn1_w_up[l], ffn1_w_down[l], mix_norm[l],
                      emit_residual=True, name="ffn1")
        q, k, v, conv, ga, gc = _in_proj(hm, w_in[l], b_gate[l], conv_w[l], bsz=bsz, seq=seq)
        attn = _moba(q, k, v, bsz=bsz, seq=seq)
        x2 = _mix_out(attn, conv, w_attn_out[l].astype(BF16), w_conv_out[l].astype(BF16), ga, gc,
                      w_out[l].astype(BF16), x1)
        if last:
            xs = _ffn(x2, ffn2_norm[l], ffn2_w_gate[l], ffn2_w_up[l], ffn2_w_down[l], final_norm,
                      emit_residual=False, name="ffn2")
        else:
            one = jnp.ones((d,), F32)
            xs, _ = _ffn(x2, ffn2_norm[l], ffn2_w_gate[l], ffn2_w_up[l], ffn2_w_down[l], one,
                         emit_residual=True, name="ffn2")
    return xs.reshape(bsz, seq, d)
```

```python
import functools

import jax
import jax.numpy as jnp
from jax import lax
from jax.experimental import pallas as pl
from jax.experimental.pallas import tpu as pltpu

F32 = jnp.float32
BF16 = jnp.bfloat16

N_HEADS = 16
HEAD_DIM = 128
CONV_KERNEL = 3
MOBA_BLOCK = 256
MOBA_TOPK = 3
ALIBI_PARTS = 3
RMS_EPS = 1e-6
N_IN_SEGMENTS = 8

V7X_VMEM_BYTES = 64 * 1024 * 1024
VMEM_LIMIT_BYTES = V7X_VMEM_BYTES - 8 * 1024 * 1024

FFN_TOKEN_TILE = 1024
FFN_DFF_TILE = 256
FFN_ROW_CHUNK = 512

IN_PROJ_CHANNEL_TILE = 128
IN_PROJ_HEADS = N_HEADS * HEAD_DIM // IN_PROJ_CHANNEL_TILE
IN_PROJ_ROW_CHUNK = 512

MIX_TOKEN_TILE = 256
MIX_CHANNEL_CHUNK = 512

LOG2_E = 1.4426950408889634
QK_LOG2_SCALE = LOG2_E * HEAD_DIM ** -0.5

MASK_NEG = -1e30


def _params(*semantics):
    return pltpu.CompilerParams(dimension_semantics=semantics, vmem_limit_bytes=VMEM_LIMIT_BYTES)


def _rms_norm(x, g):
    return x * lax.rsqrt(jnp.mean(x * x, axis=-1, keepdims=True) + RMS_EPS) * g


def _dot(a, b):
    return jnp.dot(a, b, preferred_element_type=F32)


def _dot_nt(a, b):
    return lax.dot_general(a, b, (((1,), (1,)), ((), ())), preferred_element_type=F32)


def _ffn_kernel(x_hbm, gin_ref, wg_ref, wu_ref, wd_ref, gout_ref, *rest, emit_residual, row_chunk):
    if emit_residual:
        y_hbm, n_hbm, h_ref, acc_ref, x_buf, y_buf, n_buf, sems = rest
    else:
        n_hbm, h_ref, acc_ref, x_buf, n_buf, sems = rest
        y_hbm = y_buf = None
    tm = x_buf.shape[0]
    i = pl.program_id(0)
    f = pl.program_id(1)
    n_tiles = pl.num_programs(0)

    def tile(ref, t):
        return ref.at[pl.ds(pl.multiple_of(t * tm, tm), tm), :]

    def x_copy(t):
        return pltpu.make_async_copy(tile(x_hbm, t), x_buf, sems.at[0])

    def y_copy(t):
        return pltpu.make_async_copy(y_buf, tile(y_hbm, t), sems.at[1])

    def n_copy(t):
        return pltpu.make_async_copy(n_buf, tile(n_hbm, t), sems.at[2])

    @pl.when(f == 0)
    def _():
        @pl.when(i == 0)
        def _():
            x_copy(0).start()

        x_copy(i).wait()
        x = x_buf[...]
        h_ref[...] = _rms_norm(x, gin_ref[...]).astype(BF16)
        acc_ref[...] = x

        @pl.when(i + 1 < n_tiles)
        def _():
            x_copy(i + 1).start()

    wg = wg_ref[...].astype(BF16)
    wu = wu_ref[...].astype(BF16)
    wd = wd_ref[...].astype(BF16)
    for r0 in range(0, tm, row_chunk):
        rows = pl.ds(r0, row_chunk)
        h = h_ref[rows, :]
        g = _dot(h, wg)
        u = _dot(h, wu)
        a = (0.5 * g * jax.nn.sigmoid(g) * u).astype(BF16)
        acc_ref[rows, :] += _dot(a, wd)

    @pl.when(f == pl.num_programs(1) - 1)
    def _():
        @pl.when(i > 0)
        def _():
            if emit_residual:
                y_copy(i - 1).wait()
            n_copy(i - 1).wait()

        y = acc_ref[...]
        if emit_residual:
            y_buf[...] = y
            y_copy(i).start()
        n_buf[...] = _rms_norm(y, gout_ref[...]).astype(n_buf.dtype)
        n_copy(i).start()

        @pl.when(i == n_tiles - 1)
        def _():
            if emit_residual:
                y_copy(i).wait()
            n_copy(i).wait()


def _ffn(x, g_in, wg, wu, wd, g_out, *, emit_residual, name):
    t, d = x.shape
    d_ff = wg.shape[1]
    tm, tf = FFN_TOKEN_TILE, FFN_DFF_TILE
    hbm = pl.BlockSpec(memory_space=pl.ANY)
    vec = pl.BlockSpec((1, d), lambda i, f: (0, 0))
    n_dtype = BF16 if emit_residual else F32
    n_shape = jax.ShapeDtypeStruct((t, d), n_dtype)
    if emit_residual:
        out_shape = (jax.ShapeDtypeStruct((t, d), F32), n_shape)
        out_specs = (hbm, hbm)
        out_bufs = [pltpu.VMEM((tm, d), F32), pltpu.VMEM((tm, d), n_dtype)]
    else:
        out_shape = n_shape
        out_specs = hbm
        out_bufs = [pltpu.VMEM((tm, d), n_dtype)]
    return pl.pallas_call(
        functools.partial(_ffn_kernel, emit_residual=emit_residual, row_chunk=FFN_ROW_CHUNK),
        out_shape=out_shape,
        grid=(t // tm, d_ff // tf),
        in_specs=[
            hbm, vec,
            pl.BlockSpec((d, tf), lambda i, f: (0, f)),
            pl.BlockSpec((d, tf), lambda i, f: (0, f)),
            pl.BlockSpec((tf, d), lambda i, f: (f, 0)),
            vec,
        ],
        out_specs=out_specs,
        scratch_shapes=[pltpu.VMEM((tm, d), BF16), pltpu.VMEM((tm, d), F32), pltpu.VMEM((tm, d), F32)]
        + out_bufs + [pltpu.SemaphoreType.DMA((3,))],
        compiler_params=_params("arbitrary", "arbitrary"),
        name=name,
    )(x, g_in.reshape(1, d), wg, wu, wd, g_out.reshape(1, d))


def _in_proj_kernel(w_hbm, h_ref, bga_ref, bgc_ref, cw_ref, q_ref, k_ref, v_ref, cp_ref, ga_ref, gc_ref,
                    wstage_ref, wcat_ref, u_ref, sems, *, row_chunk):
    seq, tn = q_ref.shape
    d = h_ref.shape[1]
    j = pl.program_id(0)

    def w_copy(seg, tile_idx):
        col = pl.multiple_of(seg * d + tile_idx * tn, tn)
        return pltpu.make_async_copy(w_hbm.at[:, pl.ds(col, tn)], wstage_ref.at[seg], sems.at[seg])

    @pl.when(pl.program_id(1) == 0)
    def _():
        @pl.when(j == 0)
        def _():
            for seg in range(N_IN_SEGMENTS):
                w_copy(seg, 0).start()

        for seg in range(N_IN_SEGMENTS):
            w_copy(seg, j).wait()
            wcat_ref[:, seg * tn:(seg + 1) * tn] = wstage_ref[seg].astype(BF16)

        @pl.when(j + 1 < pl.num_programs(0))
        def _():
            for seg in range(N_IN_SEGMENTS):
                w_copy(seg, j + 1).start()

    halo = 8
    u_ref[0:halo, :] = jnp.zeros((halo, tn), F32)
    cw = cw_ref[...]
    for r0 in range(0, seq, row_chunk):
        rows = pl.ds(r0, row_chunk)
        z = _dot(h_ref[rows, :], wcat_ref[...])
        zq, zk, zv, zb, zc, zx, zga, zgc = [z[:, seg * tn:(seg + 1) * tn] for seg in range(N_IN_SEGMENTS)]
        q_ref[rows, :] = (zq * QK_LOG2_SCALE).astype(BF16)
        k_ref[rows, :] = zk.astype(BF16)
        v_ref[rows, :] = zv.astype(BF16)
        ga_ref[rows, :] = jax.nn.sigmoid(zga + bga_ref[...]).astype(BF16)
        gc_ref[rows, :] = jax.nn.sigmoid(zgc + bgc_ref[...]).astype(BF16)
        u = zc * zx
        u_ref[pl.ds(halo + r0, row_chunk), :] = u
        conv = cw[2:3, :] * u
        for back in range(1, CONV_KERNEL):
            conv += cw[2 - back:3 - back, :] * u_ref[pl.ds(halo + r0 - back, row_chunk), :]
        cp_ref[rows, :] = (zb * conv).astype(BF16)


def _in_proj(hm, w_in, b_gate, conv_w, *, bsz, seq):
    t, d = hm.shape
    tn = IN_PROJ_CHANNEL_TILE
    nj = d // tn
    assert w_in.shape == (d, N_IN_SEGMENTS * d) and t == bsz * seq

    def b_spec(seg):
        return pl.BlockSpec((1, tn), lambda j, b, seg=seg: (0, seg * nj + j))

    tile = pl.BlockSpec((None, seq, tn), lambda j, b: (j, b, 0))
    out = jax.ShapeDtypeStruct((nj, t, tn), BF16)
    return pl.pallas_call(
        functools.partial(_in_proj_kernel, row_chunk=IN_PROJ_ROW_CHUNK),
        out_shape=(out, out, out, jax.ShapeDtypeStruct((t, d), BF16), out, out),
        grid=(nj, bsz),
        in_specs=[pl.BlockSpec(memory_space=pl.ANY), pl.BlockSpec((seq, d), lambda j, b: (b, 0)),
                  b_spec(0), b_spec(1), pl.BlockSpec((CONV_KERNEL, tn), lambda j, b: (0, j))],
        out_specs=(tile, tile, tile, pl.BlockSpec((seq, tn), lambda j, b: (b, j)), tile, tile),
        scratch_shapes=[pltpu.VMEM((N_IN_SEGMENTS, d, tn), F32),
                        pltpu.VMEM((d, N_IN_SEGMENTS * tn), BF16),
                        pltpu.VMEM((seq + 8, tn), F32),
                        pltpu.SemaphoreType.DMA((N_IN_SEGMENTS,))],
        compiler_params=_params("arbitrary", "arbitrary"),
        name="in_proj",
    )(w_in, hm, b_gate.reshape(1, 2 * d), b_gate.reshape(1, 2 * d), conv_w)


def _moba_kernel(kaug_ref, q_ref, k_ref, v_ref, o_ref, qa_ref, ka_ref, va_ref):
    seq, hd = q_ref.shape
    blk = MOBA_BLOCK
    nb = seq // blk

    q = q_ref[...]
    k = k_ref[...]

    kmean = jnp.mean(k.astype(F32).reshape(nb, blk, hd), axis=1)
    kmean_hi = kmean.astype(BF16)
    kmean_lo = (kmean - kmean_hi.astype(F32)).astype(BF16)
    gate2 = _dot_nt(jnp.concatenate([kmean_hi, kmean_lo], axis=0), q)
    gate = gate2[:nb] + gate2[nb:]
    blk_id = lax.broadcasted_iota(jnp.int32, (nb, seq), 0)
    q_blk = lax.broadcasted_iota(jnp.int32, (nb, seq), 1) // blk
    gate = jnp.where(blk_id < q_blk, gate, -jnp.inf)
    mask_rows = []
    for n in range(nb):
        g_n = gate[n:n + 1, :]
        beats = jnp.where(gate > g_n, 1.0, jnp.where((gate == g_n) & (blk_id < n), 1.0, 0.0))
        rank = jnp.sum(beats, axis=0, keepdims=True)
        q_blk_row = q_blk[0:1, :]
        visible = ((rank < float(MOBA_TOPK)) & (n < q_blk_row)) | (n == q_blk_row)
        mask_rows.append(jnp.where(visible, 0.0, MASK_NEG))
    mask_t = jnp.concatenate(mask_rows + [jnp.ones((ALIBI_PARTS, seq), F32),
                                          jnp.zeros((hd - nb - ALIBI_PARTS, seq), F32)], axis=0)
    mask_cols = mask_t.T.astype(BF16)

    qa_ref[:, :hd] = q
    qa_ref[:, hd:] = mask_cols
    ka_ref[:, :hd] = k
    ka_ref[:, hd:] = kaug_ref[...]
    va_ref[:, :hd] = v_ref[...]
    va_ref[:, hd:] = jnp.ones((seq, hd), BF16)

    causal = (lax.broadcasted_iota(jnp.int32, (blk, blk), 0)
              >= lax.broadcasted_iota(jnp.int32, (blk, blk), 1))

    def scores(qb):
        return _dot_nt(qa_ref[qb * blk:(qb + 1) * blk, :], ka_ref[0:(qb + 1) * blk, :])

    order = list(range(nb - 1, -1, -1))
    s_next = scores(order[0])
    for idx, qb in enumerate(order):
        s = s_next
        if idx + 1 < nb:
            s_next = scores(order[idx + 1])
        s_own = jnp.where(causal, s[:, qb * blk:], -jnp.inf)
        s = s_own if qb == 0 else jnp.concatenate([s[:, :qb * blk], s_own], axis=1)
        m = jnp.max(s, axis=-1, keepdims=True)
        p = jnp.exp2(s - m).astype(BF16)
        acc = _dot(p, va_ref[0:(qb + 1) * blk, :])
        o_ref[qb * blk:(qb + 1) * blk, :] = (acc[:, :hd] / acc[:, hd:]).astype(o_ref.dtype)


def _key_augmentation(seq):
    nb = seq // MOBA_BLOCK
    slopes = jnp.exp2(-8.0 * jnp.arange(1, N_HEADS + 1, dtype=F32) / N_HEADS)
    t_k = jnp.arange(seq, dtype=jnp.int32)[None, :, None]
    lane = lax.broadcasted_iota(jnp.int32, (1, 1, HEAD_DIM), 2)
    table = jnp.where(lane == t_k // MOBA_BLOCK, 1.0, 0.0)
    rest = LOG2_E * slopes[:, None, None] * t_k.astype(F32)
    for part in range(ALIBI_PARTS):
        top = lax.bitcast_convert_type(
            lax.bitcast_convert_type(rest, jnp.uint32) & jnp.uint32(0xFFFF0000), F32)
        table = jnp.where(lane == nb + part, top, table)
        rest = rest - top
    return table.astype(BF16)


def _moba(q, k, v, *, bsz, seq):
    n_heads, t, hd = q.shape
    assert (n_heads, hd) == (N_HEADS, HEAD_DIM) == (IN_PROJ_HEADS, IN_PROJ_CHANNEL_TILE)
    tile = pl.BlockSpec((None, seq, hd), lambda b, h: (h, b, 0))
    return pl.pallas_call(
        _moba_kernel,
        out_shape=jax.ShapeDtypeStruct((t, n_heads * hd), BF16),
        grid=(bsz, N_HEADS),
        in_specs=[pl.BlockSpec((None, seq, hd), lambda b, h: (h, 0, 0)), tile, tile, tile],
        out_specs=pl.BlockSpec((seq, hd), lambda b, h: (b, h)),
        scratch_shapes=[
            pltpu.VMEM((seq, 2 * HEAD_DIM), BF16),
            pltpu.VMEM((seq, 2 * HEAD_DIM), BF16),
            pltpu.VMEM((seq, 2 * HEAD_DIM), BF16),
        ],
        compiler_params=_params("parallel", "arbitrary"),
        name="moba",
    )(_key_augmentation(seq), q, k, v)


def _mix_out_kernel(a_ref, c_ref, ga_ref, gc_ref, x_ref, wa_hbm, wc_hbm, wo_hbm, o_ref,
                    w_ref, stage_ref, sems, *, chunk):
    d = o_ref.shape[1]
    n_chunks = d // chunk
    weights = (wa_hbm, wc_hbm, wo_hbm)

    @pl.when(pl.program_id(0) == 0)
    def _():
        def w_copy(idx):
            w, k = divmod(idx, n_chunks)
            return pltpu.make_async_copy(weights[w].at[pl.ds(k * chunk, chunk), :],
                                         stage_ref.at[idx % 2], sems.at[idx % 2])

        w_copy(0).start()
        for idx in range(len(weights) * n_chunks):
            if idx + 1 < len(weights) * n_chunks:
                w_copy(idx + 1).start()
            w_copy(idx).wait()
            w, k = divmod(idx, n_chunks)
            w_ref[w, pl.ds(k * chunk, chunk), :] = stage_ref[idx % 2].astype(BF16)

    a = a_ref[...]
    c = c_ref[...]
    acc = x_ref[...]
    for c0 in range(0, d, chunk):
        cols = pl.ds(c0, chunk)
        ya = _dot(a, w_ref[0, :, cols])
        yc = _dot(c, w_ref[1, :, cols])
        tiles = range(c0 // ga_ref.shape[2], (c0 + chunk) // ga_ref.shape[2])
        ga = jnp.concatenate([ga_ref[t] for t in tiles], axis=1).astype(F32)
        gc = jnp.concatenate([gc_ref[t] for t in tiles], axis=1).astype(F32)
        m = (ga * ya + gc * yc).astype(BF16)
        acc = acc + _dot(m, w_ref[2, cols, :])
    o_ref[...] = acc


def _mix_out(attn, conv, wa, wc, ga, gc, wo, x1):
    t, d = attn.shape
    tm, chunk = MIX_TOKEN_TILE, MIX_CHANNEL_CHUNK
    row = pl.BlockSpec((tm, d), lambda i: (i, 0))
    gate = pl.BlockSpec((ga.shape[0], tm, ga.shape[2]), lambda i: (0, i, 0))
    hbm = pl.BlockSpec(memory_space=pl.ANY)
    return pl.pallas_call(
        functools.partial(_mix_out_kernel, chunk=chunk),
        out_shape=jax.ShapeDtypeStruct((t, d), F32),
        grid=(t // tm,),
        in_specs=[row, row, gate, gate, row, hbm, hbm, hbm],
        out_specs=row,
        scratch_shapes=[pltpu.VMEM((3, d, d), BF16), pltpu.VMEM((2, chunk, d), F32),
                        pltpu.SemaphoreType.DMA((2,))],
        compiler_params=_params("arbitrary"),
        name="mix_out",
    )(attn, conv, ga, gc, x1, wa, wc, wo)


def kernel(x, ffn1_norm, ffn1_w_gate, ffn1_w_up, ffn1_w_down, mix_norm, w_in, b_gate, conv_w,
           w_attn_out, w_conv_out, w_out, ffn2_norm, ffn2_w_gate, ffn2_w_up, ffn2_w_down, final_norm):
    bsz, seq, d = x.shape
    depth = ffn1_norm.shape[0]
    assert d == N_HEADS * HEAD_DIM and seq % MOBA_BLOCK == 0 and seq // MOBA_BLOCK <= HEAD_DIM
    xs = x.reshape(bsz * seq, d)
    for l in range(depth):
        last = l == depth - 1
        x1, hm = _ffn(xs, ffn1_norm[l], ffn1_w_gate[l], ffn1_w_up[l], ffn1_w_down[l], mix_norm[l],
                      emit_residual=True, name="ffn1")
        q, k, v, conv, ga, gc = _in_proj(hm, w_in[l], b_gate[l], conv_w[l], bsz=bsz, seq=seq)
        attn = _moba(q, k, v, bsz=bsz, seq=seq)
        x2 = _mix_out(attn, conv, w_attn_out[l], w_conv_out[l], ga, gc, w_out[l], x1)
        if last:
            xs = _ffn(x2, ffn2_norm[l], ffn2_w_gate[l], ffn2_w_up[l], ffn2_w_down[l], final_norm,
                      emit_residual=False, name="ffn2")
        else:
            one = jnp.ones((d,), F32)
            xs, _ = _ffn(x2, ffn2_norm[l], ffn2_w_gate[l], ffn2_w_up[l], ffn2_w_down[l], one,
                         emit_residual=True, name="ffn2")
    return xs.reshape(bsz, seq, d)
```

```python
import functools

import jax
import jax.numpy as jnp
from jax import lax
from jax.experimental import pallas as pl
from jax.experimental.pallas import tpu as pltpu

F32 = jnp.float32
BF16 = jnp.bfloat16

N_HEADS = 16
HEAD_DIM = 128
CONV_KERNEL = 3
MOBA_BLOCK = 256
MOBA_TOPK = 3
MOBA_HEADS_PER_STEP = 4
ALIBI_PARTS = 3
RMS_EPS = 1e-6
N_IN_SEGMENTS = 8

V7X_VMEM_BYTES = 64 * 1024 * 1024
VMEM_LIMIT_BYTES = V7X_VMEM_BYTES - 8 * 1024 * 1024

FFN_TOKEN_TILE = 1024
FFN_DFF_TILE = 256
FFN_ROW_CHUNK = 512

IN_PROJ_CHANNEL_TILE = 128
IN_PROJ_HEADS = N_HEADS * HEAD_DIM // IN_PROJ_CHANNEL_TILE
IN_PROJ_ROW_CHUNK = 512

MIX_TOKEN_TILE = 256
MIX_CHANNEL_CHUNK = 512

LOG2_E = 1.4426950408889634
QK_LOG2_SCALE = LOG2_E * HEAD_DIM ** -0.5

MASK_NEG = -1e30


def _params(*semantics):
    return pltpu.CompilerParams(dimension_semantics=semantics, vmem_limit_bytes=VMEM_LIMIT_BYTES)


def _rms_norm(x, g):
    return x * lax.rsqrt(jnp.mean(x * x, axis=-1, keepdims=True) + RMS_EPS) * g


def _dot(a, b):
    return jnp.dot(a, b, preferred_element_type=F32)


def _dot_nt(a, b):
    return lax.dot_general(a, b, (((1,), (1,)), ((), ())), preferred_element_type=F32)


def _ffn_kernel(x_hbm, gin_ref, wg_ref, wu_ref, wd_ref, gout_ref, *rest, emit_residual, row_chunk):
    if emit_residual:
        y_hbm, n_hbm, h_ref, acc_ref, x_buf, y_buf, n_buf, sems = rest
    else:
        n_hbm, h_ref, acc_ref, x_buf, n_buf, sems = rest
        y_hbm = y_buf = None
    tm = x_buf.shape[0]
    i = pl.program_id(0)
    f = pl.program_id(1)
    n_tiles = pl.num_programs(0)

    def tile(ref, t):
        return ref.at[pl.ds(pl.multiple_of(t * tm, tm), tm), :]

    def x_copy(t):
        return pltpu.make_async_copy(tile(x_hbm, t), x_buf, sems.at[0])

    def y_copy(t):
        return pltpu.make_async_copy(y_buf, tile(y_hbm, t), sems.at[1])

    def n_copy(t):
        return pltpu.make_async_copy(n_buf, tile(n_hbm, t), sems.at[2])

    @pl.when(f == 0)
    def _():
        @pl.when(i == 0)
        def _():
            x_copy(0).start()

        x_copy(i).wait()
        x = x_buf[...]
        h_ref[...] = _rms_norm(x, gin_ref[...]).astype(BF16)
        acc_ref[...] = x

        @pl.when(i + 1 < n_tiles)
        def _():
            x_copy(i + 1).start()

    wg = wg_ref[...].astype(BF16)
    wu = wu_ref[...].astype(BF16)
    wd = wd_ref[...].astype(BF16)
    for r0 in range(0, tm, row_chunk):
        rows = pl.ds(r0, row_chunk)
        h = h_ref[rows, :]
        g = _dot(h, wg)
        u = _dot(h, wu)
        a = (0.5 * g * jax.nn.sigmoid(g) * u).astype(BF16)
        acc_ref[rows, :] += _dot(a, wd)

    @pl.when(f == pl.num_programs(1) - 1)
    def _():
        @pl.when(i > 0)
        def _():
            if emit_residual:
                y_copy(i - 1).wait()
            n_copy(i - 1).wait()

        y = acc_ref[...]
        if emit_residual:
            y_buf[...] = y
            y_copy(i).start()
        n_buf[...] = _rms_norm(y, gout_ref[...]).astype(n_buf.dtype)
        n_copy(i).start()

        @pl.when(i == n_tiles - 1)
        def _():
            if emit_residual:
                y_copy(i).wait()
            n_copy(i).wait()


def _ffn(x, g_in, wg, wu, wd, g_out, *, emit_residual, name):
    t, d = x.shape
    d_ff = wg.shape[1]
    tm, tf = FFN_TOKEN_TILE, FFN_DFF_TILE
    hbm = pl.BlockSpec(memory_space=pl.ANY)
    vec = pl.BlockSpec((1, d), lambda i, f: (0, 0))
    n_dtype = BF16 if emit_residual else F32
    n_shape = jax.ShapeDtypeStruct((t, d), n_dtype)
    if emit_residual:
        out_shape = (jax.ShapeDtypeStruct((t, d), F32), n_shape)
        out_specs = (hbm, hbm)
        out_bufs = [pltpu.VMEM((tm, d), F32), pltpu.VMEM((tm, d), n_dtype)]
    else:
        out_shape = n_shape
        out_specs = hbm
        out_bufs = [pltpu.VMEM((tm, d), n_dtype)]
    return pl.pallas_call(
        functools.partial(_ffn_kernel, emit_residual=emit_residual, row_chunk=FFN_ROW_CHUNK),
        out_shape=out_shape,
        grid=(t // tm, d_ff // tf),
        in_specs=[
            hbm, vec,
            pl.BlockSpec((d, tf), lambda i, f: (0, f)),
            pl.BlockSpec((d, tf), lambda i, f: (0, f)),
            pl.BlockSpec((tf, d), lambda i, f: (f, 0)),
            vec,
        ],
        out_specs=out_specs,
        scratch_shapes=[pltpu.VMEM((tm, d), BF16), pltpu.VMEM((tm, d), F32), pltpu.VMEM((tm, d), F32)]
        + out_bufs + [pltpu.SemaphoreType.DMA((3,))],
        compiler_params=_params("arbitrary", "arbitrary"),
        name=name,
    )(x, g_in.reshape(1, d), wg, wu, wd, g_out.reshape(1, d))


def _in_proj_kernel(w_hbm, h_ref, bga_ref, bgc_ref, cw_ref, q_ref, k_ref, v_ref, cp_ref, ga_ref, gc_ref,
                    wstage_ref, wcat_ref, u_ref, sems, *, row_chunk):
    seq, tn = q_ref.shape
    d = h_ref.shape[1]
    j = pl.program_id(0)

    def w_copy(seg, tile_idx):
        col = pl.multiple_of(seg * d + tile_idx * tn, tn)
        return pltpu.make_async_copy(w_hbm.at[:, pl.ds(col, tn)], wstage_ref.at[seg], sems.at[seg])

    @pl.when(pl.program_id(1) == 0)
    def _():
        @pl.when(j == 0)
        def _():
            for seg in range(N_IN_SEGMENTS):
                w_copy(seg, 0).start()

        for seg in range(N_IN_SEGMENTS):
            w_copy(seg, j).wait()
            wcat_ref[:, seg * tn:(seg + 1) * tn] = wstage_ref[seg].astype(BF16)

        @pl.when(j + 1 < pl.num_programs(0))
        def _():
            for seg in range(N_IN_SEGMENTS):
                w_copy(seg, j + 1).start()

    halo = 8
    u_ref[0:halo, :] = jnp.zeros((halo, tn), F32)
    cw = cw_ref[...]
    for r0 in range(0, seq, row_chunk):
        rows = pl.ds(r0, row_chunk)
        z = _dot(h_ref[rows, :], wcat_ref[...])
        zq, zk, zv, zb, zc, zx, zga, zgc = [z[:, seg * tn:(seg + 1) * tn] for seg in range(N_IN_SEGMENTS)]
        q_ref[rows, :] = (zq * QK_LOG2_SCALE).astype(BF16)
        k_ref[rows, :] = zk.astype(BF16)
        v_ref[rows, :] = zv.astype(BF16)
        ga_ref[rows, :] = jax.nn.sigmoid(zga + bga_ref[...]).astype(BF16)
        gc_ref[rows, :] = jax.nn.sigmoid(zgc + bgc_ref[...]).astype(BF16)
        u = zc * zx
        u_ref[pl.ds(halo + r0, row_chunk), :] = u
        conv = cw[2:3, :] * u
        for back in range(1, CONV_KERNEL):
            conv += cw[2 - back:3 - back, :] * u_ref[pl.ds(halo + r0 - back, row_chunk), :]
        cp_ref[rows, :] = (zb * conv).astype(BF16)


def _in_proj(hm, w_in, b_gate, conv_w, *, bsz, seq):
    t, d = hm.shape
    tn = IN_PROJ_CHANNEL_TILE
    nj = d // tn
    assert w_in.shape == (d, N_IN_SEGMENTS * d) and t == bsz * seq

    def b_spec(seg):
        return pl.BlockSpec((1, tn), lambda j, b, seg=seg: (0, seg * nj + j))

    tile = pl.BlockSpec((None, seq, tn), lambda j, b: (j, b, 0))
    out = jax.ShapeDtypeStruct((nj, t, tn), BF16)
    return pl.pallas_call(
        functools.partial(_in_proj_kernel, row_chunk=IN_PROJ_ROW_CHUNK),
        out_shape=(out, out, out, jax.ShapeDtypeStruct((t, d), BF16), out, out),
        grid=(nj, bsz),
        in_specs=[pl.BlockSpec(memory_space=pl.ANY), pl.BlockSpec((seq, d), lambda j, b: (b, 0)),
                  b_spec(0), b_spec(1), pl.BlockSpec((CONV_KERNEL, tn), lambda j, b: (0, j))],
        out_specs=(tile, tile, tile, pl.BlockSpec((seq, tn), lambda j, b: (b, j)), tile, tile),
        scratch_shapes=[pltpu.VMEM((N_IN_SEGMENTS, d, tn), F32),
                        pltpu.VMEM((d, N_IN_SEGMENTS * tn), BF16),
                        pltpu.VMEM((seq + 8, tn), F32),
                        pltpu.SemaphoreType.DMA((N_IN_SEGMENTS,))],
        compiler_params=_params("arbitrary", "arbitrary"),
        name="in_proj",
    )(w_in, hm, b_gate.reshape(1, 2 * d), b_gate.reshape(1, 2 * d), conv_w)


def _moba_kernel(kaug_ref, q_ref, k_ref, v_ref, o_ref, qa_ref, ka_ref, va_ref):
    heads, seq, hd = q_ref.shape
    blk = MOBA_BLOCK
    nb = seq // blk

    def gating(hh):
        q = q_ref[hh]
        k = k_ref[hh]
        kmean = jnp.mean(k.astype(F32).reshape(nb, blk, hd), axis=1)
        kmean_hi = kmean.astype(BF16)
        kmean_lo = (kmean - kmean_hi.astype(F32)).astype(BF16)
        gate2 = _dot_nt(jnp.concatenate([kmean_hi, kmean_lo], axis=0), q)
        gate = gate2[:nb] + gate2[nb:]
        blk_id = lax.broadcasted_iota(jnp.int32, (nb, seq), 0)
        q_blk = lax.broadcasted_iota(jnp.int32, (nb, seq), 1) // blk
        gate = jnp.where(blk_id < q_blk, gate, -jnp.inf)
        mask_rows = []
        for n in range(nb):
            g_n = gate[n:n + 1, :]
            beats = jnp.where(gate > g_n, 1.0, jnp.where((gate == g_n) & (blk_id < n), 1.0, 0.0))
            rank = jnp.sum(beats, axis=0, keepdims=True)
            q_blk_row = q_blk[0:1, :]
            visible = ((rank < float(MOBA_TOPK)) & (n < q_blk_row)) | (n == q_blk_row)
            mask_rows.append(jnp.where(visible, 0.0, MASK_NEG))
        mask_t = jnp.concatenate(mask_rows + [jnp.ones((ALIBI_PARTS, seq), F32),
                                              jnp.zeros((hd - nb - ALIBI_PARTS, seq), F32)], axis=0)
        qa_ref[hh, :, :hd] = q
        qa_ref[hh, :, hd:] = mask_t.T.astype(BF16)
        ka_ref[hh, :, :hd] = k
        ka_ref[hh, :, hd:] = kaug_ref[hh]
        va_ref[hh, :, :hd] = v_ref[hh]
        va_ref[hh, :, hd:] = jnp.ones((seq, hd), BF16)

    causal = (lax.broadcasted_iota(jnp.int32, (blk, blk), 0)
              >= lax.broadcasted_iota(jnp.int32, (blk, blk), 1))

    def scores(hh, qb):
        return _dot_nt(qa_ref[hh, qb * blk:(qb + 1) * blk, :], ka_ref[hh, 0:(qb + 1) * blk, :])

    def attention(hh, after_first_scores):
        order = list(range(nb - 1, -1, -1))
        s_next = scores(hh, order[0])
        after_first_scores()
        for idx, qb in enumerate(order):
            s = s_next
            if idx + 1 < nb:
                s_next = scores(hh, order[idx + 1])
            s_own = jnp.where(causal, s[:, qb * blk:], -jnp.inf)
            s = s_own if qb == 0 else jnp.concatenate([s[:, :qb * blk], s_own], axis=1)
            m = jnp.max(s, axis=-1, keepdims=True)
            p = jnp.exp2(s - m).astype(BF16)
            acc = _dot(p, va_ref[hh, 0:(qb + 1) * blk, :])
            o_ref[qb * blk:(qb + 1) * blk, hh * hd:(hh + 1) * hd] = (acc[:, :hd] / acc[:, hd:]).astype(o_ref.dtype)

    gating(0)
    for hh in range(heads):
        attention(hh, (lambda nxt=hh + 1: gating(nxt)) if hh + 1 < heads else (lambda: None))


def _key_augmentation(seq):
    nb = seq // MOBA_BLOCK
    slopes = jnp.exp2(-8.0 * jnp.arange(1, N_HEADS + 1, dtype=F32) / N_HEADS)
    t_k = jnp.arange(seq, dtype=jnp.int32)[None, :, None]
    lane = lax.broadcasted_iota(jnp.int32, (1, 1, HEAD_DIM), 2)
    table = jnp.where(lane == t_k // MOBA_BLOCK, 1.0, 0.0)
    rest = LOG2_E * slopes[:, None, None] * t_k.astype(F32)
    for part in range(ALIBI_PARTS):
        top = lax.bitcast_convert_type(
            lax.bitcast_convert_type(rest, jnp.uint32) & jnp.uint32(0xFFFF0000), F32)
        table = jnp.where(lane == nb + part, top, table)
        rest = rest - top
    return table.astype(BF16)


def _moba(q, k, v, *, bsz, seq):
    n_heads, t, hd = q.shape
    assert (n_heads, hd) == (N_HEADS, HEAD_DIM) == (IN_PROJ_HEADS, IN_PROJ_CHANNEL_TILE)
    hps = MOBA_HEADS_PER_STEP
    tile = pl.BlockSpec((hps, seq, hd), lambda b, h: (h, b, 0))
    return pl.pallas_call(
        _moba_kernel,
        out_shape=jax.ShapeDtypeStruct((t, n_heads * hd), BF16),
        grid=(bsz, N_HEADS // hps),
        in_specs=[pl.BlockSpec((hps, seq, hd), lambda b, h: (h, 0, 0)), tile, tile, tile],
        out_specs=pl.BlockSpec((seq, hps * hd), lambda b, h: (b, h)),
        scratch_shapes=[pltpu.VMEM((hps, seq, 2 * hd), BF16)] * 3,
        compiler_params=_params("parallel", "arbitrary"),
        name="moba",
    )(_key_augmentation(seq), q, k, v)


def _mix_out_kernel(a_ref, c_ref, ga_ref, gc_ref, x_ref, wa_hbm, wc_hbm, wo_hbm, o_ref,
                    w_ref, stage_ref, sems, *, chunk):
    d = o_ref.shape[1]
    n_chunks = d // chunk
    weights = (wa_hbm, wc_hbm, wo_hbm)

    @pl.when(pl.program_id(0) == 0)
    def _():
        def w_copy(idx):
            w, k = divmod(idx, n_chunks)
            return pltpu.make_async_copy(weights[w].at[pl.ds(k * chunk, chunk), :],
                                         stage_ref.at[idx % 2], sems.at[idx % 2])

        w_copy(0).start()
        for idx in range(len(weights) * n_chunks):
            if idx + 1 < len(weights) * n_chunks:
                w_copy(idx + 1).start()
            w_copy(idx).wait()
            w, k = divmod(idx, n_chunks)
            w_ref[w, pl.ds(k * chunk, chunk), :] = stage_ref[idx % 2].astype(BF16)

    a = a_ref[...]
    c = c_ref[...]
    acc = x_ref[...]
    for c0 in range(0, d, chunk):
        cols = pl.ds(c0, chunk)
        ya = _dot(a, w_ref[0, :, cols])
        yc = _dot(c, w_ref[1, :, cols])
        tiles = range(c0 // ga_ref.shape[2], (c0 + chunk) // ga_ref.shape[2])
        ga = jnp.concatenate([ga_ref[t] for t in tiles], axis=1).astype(F32)
        gc = jnp.concatenate([gc_ref[t] for t in tiles], axis=1).astype(F32)
        m = (ga * ya + gc * yc).astype(BF16)
        acc = acc + _dot(m, w_ref[2, cols, :])
    o_ref[...] = acc


def _mix_out(attn, conv, wa, wc, ga, gc, wo, x1):
    t, d = attn.shape
    tm, chunk = MIX_TOKEN_TILE, MIX_CHANNEL_CHUNK
    row = pl.BlockSpec((tm, d), lambda i: (i, 0))
    gate = pl.BlockSpec((ga.shape[0], tm, ga.shape[2]), lambda i: (0, i, 0))
    hbm = pl.BlockSpec(memory_space=pl.ANY)
    return pl.pallas_call(
        functools.partial(_mix_out_kernel, chunk=chunk),
        out_shape=jax.ShapeDtypeStruct((t, d), F32),
        grid=(t // tm,),
        in_specs=[row, row, gate, gate, row, hbm, hbm, hbm],
        out_specs=row,
        scratch_shapes=[pltpu.VMEM((3, d, d), BF16), pltpu.VMEM((2, chunk, d), F32),
                        pltpu.SemaphoreType.DMA((2,))],
        compiler_params=_params("arbitrary"),
        name="mix_out",
    )(attn, conv, ga, gc, x1, wa, wc, wo)


def kernel(x, ffn1_norm, ffn1_w_gate, ffn1_w_up, ffn1_w_down, mix_norm, w_in, b_gate, conv_w,
           w_attn_out, w_conv_out, w_out, ffn2_norm, ffn2_w_gate, ffn2_w_up, ffn2_w_down, final_norm):
    bsz, seq, d = x.shape
    depth = ffn1_norm.shape[0]
    assert d == N_HEADS * HEAD_DIM and seq % MOBA_BLOCK == 0 and seq // MOBA_BLOCK <= HEAD_DIM
    xs = x.reshape(bsz * seq, d)
    for l in range(depth):
        last = l == depth - 1
        x1, hm = _ffn(xs, ffn1_norm[l], ffn1_w_gate[l], ffn1_w_up[l], ffn1_w_down[l], mix_norm[l],
                      emit_residual=True, name="ffn1")
        q, k, v, conv, ga, gc = _in_proj(hm, w_in[l], b_gate[l], conv_w[l], bsz=bsz, seq=seq)
        attn = _moba(q, k, v, bsz=bsz, seq=seq)
        x2 = _mix_out(attn, conv, w_attn_out[l], w_conv_out[l], ga, gc, w_out[l], x1)
        if last:
            xs = _ffn(x2, ffn2_norm[l], ffn2_w_gate[l], ffn2_w_up[l], ffn2_w_down[l], final_norm,
                      emit_residual=False, name="ffn2")
        else:
            one = jnp.ones((d,), F32)
            xs, _ = _ffn(x2, ffn2_norm[l], ffn2_w_gate[l], ffn2_w_up[l], ffn2_w_down[l], one,
                         emit_residual=True, name="ffn2")
    return xs.reshape(bsz, seq, d)
```

```python
import functools

import jax
import jax.numpy as jnp
from jax import lax
from jax.experimental import pallas as pl
from jax.experimental.pallas import tpu as pltpu

F32 = jnp.float32
BF16 = jnp.bfloat16

N_HEADS = 16
HEAD_DIM = 128
CONV_KERNEL = 3
MOBA_BLOCK = 256
MOBA_TOPK = 3
MOBA_HEADS_PER_STEP = 4
ALIBI_SLOPE_PARTS = 3
ALIBI_POS_PARTS = 2
RMS_EPS = 1e-6
N_IN_SEGMENTS = 8

V7X_VMEM_BYTES = 64 * 1024 * 1024
VMEM_LIMIT_BYTES = V7X_VMEM_BYTES - 8 * 1024 * 1024

FFN_TOKEN_TILE = 1024
FFN_DFF_TILE = 256
FFN_ROW_CHUNK = 512

IN_PROJ_CHANNEL_TILE = 128
IN_PROJ_HEADS = N_HEADS * HEAD_DIM // IN_PROJ_CHANNEL_TILE
IN_PROJ_ROW_CHUNK = 512

MIX_TOKEN_TILE = 256
MIX_CHANNEL_CHUNK = 512

LOG2_E = 1.4426950408889634
QK_LOG2_SCALE = LOG2_E * HEAD_DIM ** -0.5

MASK_NEG = -1e30


def _params(*semantics):
    return pltpu.CompilerParams(dimension_semantics=semantics, vmem_limit_bytes=VMEM_LIMIT_BYTES)


def _rms_norm(x, g):
    return x * lax.rsqrt(jnp.mean(x * x, axis=-1, keepdims=True) + RMS_EPS) * g


def _dot(a, b):
    return jnp.dot(a, b, preferred_element_type=F32)


def _dot_nt(a, b):
    return lax.dot_general(a, b, (((1,), (1,)), ((), ())), preferred_element_type=F32)


def _ffn_kernel(x_hbm, gin_ref, wg_ref, wu_ref, wd_ref, gout_ref, *rest, emit_residual, row_chunk):
    if emit_residual:
        y_hbm, n_hbm, h_ref, acc_ref, x_buf, y_buf, n_buf, sems = rest
    else:
        n_hbm, h_ref, acc_ref, x_buf, n_buf, sems = rest
        y_hbm = y_buf = None
    tm = x_buf.shape[0]
    i = pl.program_id(0)
    f = pl.program_id(1)
    n_tiles = pl.num_programs(0)

    def tile(ref, t):
        return ref.at[pl.ds(pl.multiple_of(t * tm, tm), tm), :]

    def x_copy(t):
        return pltpu.make_async_copy(tile(x_hbm, t), x_buf, sems.at[0])

    def y_copy(t):
        return pltpu.make_async_copy(y_buf, tile(y_hbm, t), sems.at[1])

    def n_copy(t):
        return pltpu.make_async_copy(n_buf, tile(n_hbm, t), sems.at[2])

    @pl.when(f == 0)
    def _():
        @pl.when(i == 0)
        def _():
            x_copy(0).start()

        x_copy(i).wait()
        x = x_buf[...]
        h_ref[...] = _rms_norm(x, gin_ref[...]).astype(BF16)
        acc_ref[...] = x

        @pl.when(i + 1 < n_tiles)
        def _():
            x_copy(i + 1).start()

    wg = wg_ref[...].astype(BF16)
    wu = wu_ref[...].astype(BF16)
    wd = wd_ref[...].astype(BF16)
    for r0 in range(0, tm, row_chunk):
        rows = pl.ds(r0, row_chunk)
        h = h_ref[rows, :]
        g = _dot(h, wg)
        u = _dot(h, wu)
        a = (0.5 * g * jax.nn.sigmoid(g) * u).astype(BF16)
        acc_ref[rows, :] += _dot(a, wd)

    @pl.when(f == pl.num_programs(1) - 1)
    def _():
        @pl.when(i > 0)
        def _():
            if emit_residual:
                y_copy(i - 1).wait()
            n_copy(i - 1).wait()

        y = acc_ref[...]
        if emit_residual:
            y_buf[...] = y
            y_copy(i).start()
        n_buf[...] = _rms_norm(y, gout_ref[...]).astype(n_buf.dtype)
        n_copy(i).start()

        @pl.when(i == n_tiles - 1)
        def _():
            if emit_residual:
                y_copy(i).wait()
            n_copy(i).wait()


def _ffn(x, g_in, wg, wu, wd, g_out, *, emit_residual, name):
    t, d = x.shape
    d_ff = wg.shape[1]
    tm, tf = FFN_TOKEN_TILE, FFN_DFF_TILE
    hbm = pl.BlockSpec(memory_space=pl.ANY)
    vec = pl.BlockSpec((1, d), lambda i, f: (0, 0))
    n_dtype = BF16 if emit_residual else F32
    n_shape = jax.ShapeDtypeStruct((t, d), n_dtype)
    if emit_residual:
        out_shape = (jax.ShapeDtypeStruct((t, d), F32), n_shape)
        out_specs = (hbm, hbm)
        out_bufs = [pltpu.VMEM((tm, d), F32), pltpu.VMEM((tm, d), n_dtype)]
    else:
        out_shape = n_shape
        out_specs = hbm
        out_bufs = [pltpu.VMEM((tm, d), n_dtype)]
    return pl.pallas_call(
        functools.partial(_ffn_kernel, emit_residual=emit_residual, row_chunk=FFN_ROW_CHUNK),
        out_shape=out_shape,
        grid=(t // tm, d_ff // tf),
        in_specs=[
            hbm, vec,
            pl.BlockSpec((d, tf), lambda i, f: (0, f)),
            pl.BlockSpec((d, tf), lambda i, f: (0, f)),
            pl.BlockSpec((tf, d), lambda i, f: (f, 0)),
            vec,
        ],
        out_specs=out_specs,
        scratch_shapes=[pltpu.VMEM((tm, d), BF16), pltpu.VMEM((tm, d), F32), pltpu.VMEM((tm, d), F32)]
        + out_bufs + [pltpu.SemaphoreType.DMA((3,))],
        compiler_params=_params("arbitrary", "arbitrary"),
        name=name,
    )(x, g_in.reshape(1, d), wg, wu, wd, g_out.reshape(1, d))


def _in_proj_kernel(w_hbm, h_ref, bga_ref, bgc_ref, cw_ref, q_ref, k_ref, v_ref, cp_ref, ga_ref, gc_ref,
                    wstage_ref, wcat_ref, u_ref, sems, *, row_chunk):
    seq, tn = q_ref.shape
    d = h_ref.shape[1]
    j = pl.program_id(0)

    def w_copy(seg, tile_idx):
        col = pl.multiple_of(seg * d + tile_idx * tn, tn)
        return pltpu.make_async_copy(w_hbm.at[:, pl.ds(col, tn)], wstage_ref.at[seg], sems.at[seg])

    @pl.when(pl.program_id(1) == 0)
    def _():
        @pl.when(j == 0)
        def _():
            for seg in range(N_IN_SEGMENTS):
                w_copy(seg, 0).start()

        for seg in range(N_IN_SEGMENTS):
            w_copy(seg, j).wait()
            wcat_ref[:, seg * tn:(seg + 1) * tn] = wstage_ref[seg].astype(BF16)

        @pl.when(j + 1 < pl.num_programs(0))
        def _():
            for seg in range(N_IN_SEGMENTS):
                w_copy(seg, j + 1).start()

    halo = 8
    u_ref[0:halo, :] = jnp.zeros((halo, tn), F32)
    cw = cw_ref[...]
    for r0 in range(0, seq, row_chunk):
        rows = pl.ds(r0, row_chunk)
        z = _dot(h_ref[rows, :], wcat_ref[...])
        zq, zk, zv, zb, zc, zx, zga, zgc = [z[:, seg * tn:(seg + 1) * tn] for seg in range(N_IN_SEGMENTS)]
        q_ref[rows, :] = (zq * QK_LOG2_SCALE).astype(BF16)
        k_ref[rows, :] = zk.astype(BF16)
        v_ref[rows, :] = zv.astype(BF16)
        ga_ref[rows, :] = jax.nn.sigmoid(zga + bga_ref[...]).astype(BF16)
        gc_ref[rows, :] = jax.nn.sigmoid(zgc + bgc_ref[...]).astype(BF16)
        u = zc * zx
        u_ref[pl.ds(halo + r0, row_chunk), :] = u
        conv = cw[2:3, :] * u
        for back in range(1, CONV_KERNEL):
            conv += cw[2 - back:3 - back, :] * u_ref[pl.ds(halo + r0 - back, row_chunk), :]
        cp_ref[rows, :] = (zb * conv).astype(BF16)


def _in_proj(hm, w_in, b_gate, conv_w, *, bsz, seq):
    t, d = hm.shape
    tn = IN_PROJ_CHANNEL_TILE
    nj = d // tn
    assert w_in.shape == (d, N_IN_SEGMENTS * d) and t == bsz * seq

    def b_spec(seg):
        return pl.BlockSpec((1, tn), lambda j, b, seg=seg: (0, seg * nj + j))

    tile = pl.BlockSpec((None, seq, tn), lambda j, b: (j, b, 0))
    out = jax.ShapeDtypeStruct((nj, t, tn), BF16)
    return pl.pallas_call(
        functools.partial(_in_proj_kernel, row_chunk=IN_PROJ_ROW_CHUNK),
        out_shape=(out, out, out, jax.ShapeDtypeStruct((t, d), BF16), out, out),
        grid=(nj, bsz),
        in_specs=[pl.BlockSpec(memory_space=pl.ANY), pl.BlockSpec((seq, d), lambda j, b: (b, 0)),
                  b_spec(0), b_spec(1), pl.BlockSpec((CONV_KERNEL, tn), lambda j, b: (0, j))],
        out_specs=(tile, tile, tile, pl.BlockSpec((seq, tn), lambda j, b: (b, j)), tile, tile),
        scratch_shapes=[pltpu.VMEM((N_IN_SEGMENTS, d, tn), F32),
                        pltpu.VMEM((d, N_IN_SEGMENTS * tn), BF16),
                        pltpu.VMEM((seq + 8, tn), F32),
                        pltpu.SemaphoreType.DMA((N_IN_SEGMENTS,))],
        compiler_params=_params("arbitrary", "arbitrary"),
        name="in_proj",
    )(w_in, hm, b_gate.reshape(1, 2 * d), b_gate.reshape(1, 2 * d), conv_w)


def _moba_kernel(slope_ref, kaug_ref, q_ref, k_ref, v_ref, o_ref, qa_ref, ka_ref, va_ref):
    heads, seq, hd = q_ref.shape
    blk = MOBA_BLOCK
    nb = seq // blk

    @pl.when((pl.program_id(0) == 0) & (pl.program_id(1) == 0))
    def _():
        for hh in range(heads):
            ka_ref[hh, :, hd:] = kaug_ref[...]
            va_ref[hh, :, hd:] = jnp.ones((seq, hd), BF16)

    def gating(hh):
        q = q_ref[hh]
        k = k_ref[hh]
        kmean = jnp.mean(k.astype(F32).reshape(nb, blk, hd), axis=1)
        kmean_hi = kmean.astype(BF16)
        kmean_lo = (kmean - kmean_hi.astype(F32)).astype(BF16)
        gate2 = _dot_nt(jnp.concatenate([kmean_hi, kmean_lo], axis=0), q)
        gate = gate2[:nb] + gate2[nb:]
        blk_id = lax.broadcasted_iota(jnp.int32, (nb, seq), 0)
        q_blk = lax.broadcasted_iota(jnp.int32, (nb, seq), 1) // blk
        gate = jnp.where(blk_id < q_blk, gate, -jnp.inf)
        mask_rows = []
        for n in range(nb):
            g_n = gate[n:n + 1, :]
            beats = jnp.where(gate > g_n, 1.0, jnp.where((gate == g_n) & (blk_id < n), 1.0, 0.0))
            rank = jnp.sum(beats, axis=0, keepdims=True)
            q_blk_row = q_blk[0:1, :]
            visible = ((rank < float(MOBA_TOPK)) & (n < q_blk_row)) | (n == q_blk_row)
            mask_rows.append(jnp.where(visible, 0.0, MASK_NEG))
        head = pl.program_id(1) * heads + hh
        slope_rows = [jnp.full((1, seq), slope_ref[head * ALIBI_SLOPE_PARTS + part], F32)
                      for part in range(ALIBI_SLOPE_PARTS) for _ in range(ALIBI_POS_PARTS)]
        n_ext = nb + len(slope_rows)
        mask_t = jnp.concatenate(mask_rows + slope_rows + [jnp.zeros((hd - n_ext, seq), F32)], axis=0)
        qa_ref[hh, :, :hd] = q
        qa_ref[hh, :, hd:] = mask_t.T.astype(BF16)
        ka_ref[hh, :, :hd] = k
        va_ref[hh, :, :hd] = v_ref[hh]

    causal = (lax.broadcasted_iota(jnp.int32, (blk, blk), 0)
              >= lax.broadcasted_iota(jnp.int32, (blk, blk), 1))

    def scores(hh, qb):
        return _dot_nt(qa_ref[hh, qb * blk:(qb + 1) * blk, :], ka_ref[hh, 0:(qb + 1) * blk, :])

    def attention(hh, after_first_scores):
        order = list(range(nb - 1, -1, -1))
        s_next = scores(hh, order[0])
        after_first_scores()
        for idx, qb in enumerate(order):
            s = s_next
            if idx + 1 < nb:
                s_next = scores(hh, order[idx + 1])
            s_own = jnp.where(causal, s[:, qb * blk:], -jnp.inf)
            s = s_own if qb == 0 else jnp.concatenate([s[:, :qb * blk], s_own], axis=1)
            m = jnp.max(s, axis=-1, keepdims=True)
            p = jnp.exp2(s - m).astype(BF16)
            acc = _dot(p, va_ref[hh, 0:(qb + 1) * blk, :])
            o_ref[qb * blk:(qb + 1) * blk, hh * hd:(hh + 1) * hd] = (acc[:, :hd] / acc[:, hd:]).astype(o_ref.dtype)

    gating(0)
    for hh in range(heads):
        attention(hh, (lambda nxt=hh + 1: gating(nxt)) if hh + 1 < heads else (lambda: None))


def _alibi_slope_parts():
    rest = LOG2_E * jnp.exp2(-8.0 * jnp.arange(1, N_HEADS + 1, dtype=F32) / N_HEADS)
    parts = []
    for _ in range(ALIBI_SLOPE_PARTS):
        top = lax.bitcast_convert_type(
            lax.bitcast_convert_type(rest, jnp.uint32) & jnp.uint32(0xFFFF0000), F32)
        parts.append(top)
        rest = rest - top
    return jnp.stack(parts, axis=1).reshape(-1)


def _key_augmentation(seq):
    nb = seq // MOBA_BLOCK
    t_k = jnp.arange(seq, dtype=jnp.int32)[:, None]
    lane = lax.broadcasted_iota(jnp.int32, (1, HEAD_DIM), 1)
    pos_parts = ((t_k // MOBA_BLOCK) * MOBA_BLOCK, t_k % MOBA_BLOCK)
    assert len(pos_parts) == ALIBI_POS_PARTS and MOBA_BLOCK <= 256 and nb <= 256
    table = jnp.where(lane == t_k // MOBA_BLOCK, 1, 0)
    for col in range(ALIBI_SLOPE_PARTS * ALIBI_POS_PARTS):
        table = jnp.where(lane == nb + col, pos_parts[col % ALIBI_POS_PARTS], table)
    return table.astype(BF16)


def _moba(q, k, v, *, bsz, seq):
    n_heads, t, hd = q.shape
    assert (n_heads, hd) == (N_HEADS, HEAD_DIM) == (IN_PROJ_HEADS, IN_PROJ_CHANNEL_TILE)
    hps = MOBA_HEADS_PER_STEP
    tile = pl.BlockSpec((hps, seq, hd), lambda b, h, slopes: (h, b, 0))
    return pl.pallas_call(
        _moba_kernel,
        out_shape=jax.ShapeDtypeStruct((t, n_heads * hd), BF16),
        grid_spec=pltpu.PrefetchScalarGridSpec(
            num_scalar_prefetch=1,
            grid=(bsz, N_HEADS // hps),
            in_specs=[pl.BlockSpec((seq, hd), lambda b, h, slopes: (0, 0)), tile, tile, tile],
            out_specs=pl.BlockSpec((seq, hps * hd), lambda b, h, slopes: (b, h)),
            scratch_shapes=[pltpu.VMEM((hps, seq, 2 * hd), BF16)] * 3,
        ),
        compiler_params=_params("arbitrary", "arbitrary"),
        name="moba",
    )(_alibi_slope_parts(), _key_augmentation(seq), q, k, v)


def _mix_out_kernel(a_ref, c_ref, ga_ref, gc_ref, x_ref, wa_hbm, wc_hbm, wo_hbm, o_ref,
                    w_ref, stage_ref, sems, *, chunk):
    d = o_ref.shape[1]
    n_chunks = d // chunk
    weights = (wa_hbm, wc_hbm, wo_hbm)

    @pl.when(pl.program_id(0) == 0)
    def _():
        def w_copy(idx):
            w, k = divmod(idx, n_chunks)
            return pltpu.make_async_copy(weights[w].at[pl.ds(k * chunk, chunk), :],
                                         stage_ref.at[idx % 2], sems.at[idx % 2])

        w_copy(0).start()
        for idx in range(len(weights) * n_chunks):
            if idx + 1 < len(weights) * n_chunks:
                w_copy(idx + 1).start()
            w_copy(idx).wait()
            w, k = divmod(idx, n_chunks)
            w_ref[w, pl.ds(k * chunk, chunk), :] = stage_ref[idx % 2].astype(BF16)

    a = a_ref[...]
    c = c_ref[...]
    acc = x_ref[...]
    for c0 in range(0, d, chunk):
        cols = pl.ds(c0, chunk)
        ya = _dot(a, w_ref[0, :, cols])
        yc = _dot(c, w_ref[1, :, cols])
        tiles = range(c0 // ga_ref.shape[2], (c0 + chunk) // ga_ref.shape[2])
        ga = jnp.concatenate([ga_ref[t] for t in tiles], axis=1).astype(F32)
        gc = jnp.concatenate([gc_ref[t] for t in tiles], axis=1).astype(F32)
        m = (ga * ya + gc * yc).astype(BF16)
        acc = acc + _dot(m, w_ref[2, cols, :])
    o_ref[...] = acc


def _mix_out(attn, conv, wa, wc, ga, gc, wo, x1):
    t, d = attn.shape
    tm, chunk = MIX_TOKEN_TILE, MIX_CHANNEL_CHUNK
    row = pl.BlockSpec((tm, d), lambda i: (i, 0))
    gate = pl.BlockSpec((ga.shape[0], tm, ga.shape[2]), lambda i: (0, i, 0))
    hbm = pl.BlockSpec(memory_space=pl.ANY)
    return pl.pallas_call(
        functools.partial(_mix_out_kernel, chunk=chunk),
        out_shape=jax.ShapeDtypeStruct((t, d), F32),
        grid=(t // tm,),
        in_specs=[row, row, gate, gate, row, hbm, hbm, hbm],
        out_specs=row,
        scratch_shapes=[pltpu.VMEM((3, d, d), BF16), pltpu.VMEM((2, chunk, d), F32),
                        pltpu.SemaphoreType.DMA((2,))],
        compiler_params=_params("arbitrary"),
        name="mix_out",
    )(attn, conv, ga, gc, x1, wa, wc, wo)


def kernel(x, ffn1_norm, ffn1_w_gate, ffn1_w_up, ffn1_w_down, mix_norm, w_in, b_gate, conv_w,
           w_attn_out, w_conv_out, w_out, ffn2_norm, ffn2_w_gate, ffn2_w_up, ffn2_w_down, final_norm):
    bsz, seq, d = x.shape
    depth = ffn1_norm.shape[0]
    assert d == N_HEADS * HEAD_DIM and seq % MOBA_BLOCK == 0 and seq // MOBA_BLOCK <= HEAD_DIM
    xs = x.reshape(bsz * seq, d)
    for l in range(depth):
        last = l == depth - 1
        x1, hm = _ffn(xs, ffn1_norm[l], ffn1_w_gate[l], ffn1_w_up[l], ffn1_w_down[l], mix_norm[l],
                      emit_residual=True, name="ffn1")
        q, k, v, conv, ga, gc = _in_proj(hm, w_in[l], b_gate[l], conv_w[l], bsz=bsz, seq=seq)
        attn = _moba(q, k, v, bsz=bsz, seq=seq)
        x2 = _mix_out(attn, conv, w_attn_out[l], w_conv_out[l], ga, gc, w_out[l], x1)
        if last:
            xs = _ffn(x2, ffn2_norm[l], ffn2_w_gate[l], ffn2_w_up[l], ffn2_w_down[l], final_norm,
                      emit_residual=False, name="ffn2")
        else:
            one = jnp.ones((d,), F32)
            xs, _ = _ffn(x2, ffn2_norm[l], ffn2_w_gate[l], ffn2_w_up[l], ffn2_w_down[l], one,
                         emit_residual=True, name="ffn2")
    return xs.reshape(bsz, seq, d)
```

```python
import functools

import jax
import jax.numpy as jnp
from jax import lax
from jax.experimental import pallas as pl
from jax.experimental.pallas import tpu as pltpu

F32 = jnp.float32
BF16 = jnp.bfloat16

N_HEADS = 16
HEAD_DIM = 128
CONV_KERNEL = 3
MOBA_BLOCK = 256
MOBA_TOPK = 3
MOBA_HEADS_PER_STEP = 4
ALIBI_SLOPE_PARTS = 3
ALIBI_POS_PARTS = 2
RMS_EPS = 1e-6
N_IN_SEGMENTS = 8

F32_SUBLANES = 8
V7X_VMEM_BYTES = 64 * 1024 * 1024
VMEM_LIMIT_BYTES = V7X_VMEM_BYTES - 8 * 1024 * 1024

FFN_TOKEN_TILE = 1024
FFN_DFF_TILE = 256
FFN_ROW_CHUNK = 512
FFN_X_PREFETCH_STEP = 2

IN_PROJ_CHANNEL_TILE = 128
IN_PROJ_HEADS = N_HEADS * HEAD_DIM // IN_PROJ_CHANNEL_TILE
IN_PROJ_ROW_CHUNK = 512

MIX_TOKEN_TILE = 256
MIX_CHANNEL_CHUNK = 512

LOG2_E = 1.4426950408889634
QK_LOG2_SCALE = LOG2_E * HEAD_DIM ** -0.5

MASK_NEG = -1e30


def _params(*semantics):
    return pltpu.CompilerParams(dimension_semantics=semantics, vmem_limit_bytes=VMEM_LIMIT_BYTES)


def _rms_norm(x, g):
    return x * lax.rsqrt(jnp.mean(x * x, axis=-1, keepdims=True) + RMS_EPS) * g


def _dot(a, b):
    return jnp.dot(a, b, preferred_element_type=F32)


def _dot_nt(a, b):
    return lax.dot_general(a, b, (((1,), (1,)), ((), ())), preferred_element_type=F32)


def _ffn_kernel(x_hbm, gin_ref, wg_ref, wu_ref, wd_ref, gout_ref, *rest, emit_residual, row_chunk):
    if emit_residual:
        y_hbm, n_hbm, h_ref, acc_a, acc_b, n_buf, sems = rest
    else:
        n_hbm, h_ref, acc_a, acc_b, sems = rest
        y_hbm = n_buf = None
    tm = h_ref.shape[0]
    i = pl.program_id(0)
    f = pl.program_id(1)
    n_tiles = pl.num_programs(0)
    has_prev = i > 0
    has_next = i + 1 < n_tiles

    def hbm_tile(ref, t):
        return ref.at[pl.ds(pl.multiple_of(t * tm, tm), tm), :]

    def x_copy(t, acc):
        return pltpu.make_async_copy(hbm_tile(x_hbm, t), acc, sems.at[0])

    def acc_out_copy(t, acc):
        return pltpu.make_async_copy(acc, hbm_tile(y_hbm if emit_residual else n_hbm, t), sems.at[1])

    def n_copy(t):
        return pltpu.make_async_copy(n_buf, hbm_tile(n_hbm, t), sems.at[2])

    def step(acc_cur, acc_oth):
        @pl.when(f == 0)
        def _():
            @pl.when(i == 0)
            def _():
                x_copy(0, acc_cur).start()

            x_copy(i, acc_cur).wait()
            h_ref[...] = _rms_norm(acc_cur[...], gin_ref[...]).astype(BF16)

        @pl.when(f == FFN_X_PREFETCH_STEP)
        def _():
            @pl.when(has_prev)
            def _():
                acc_out_copy(i - 1, acc_oth).wait()

            @pl.when(has_next)
            def _():
                x_copy(i + 1, acc_oth).start()

        wg = wg_ref[...].astype(BF16)
        wu = wu_ref[...].astype(BF16)
        wd = wd_ref[...].astype(BF16)
        for r0 in range(0, tm, row_chunk):
            rows = pl.ds(r0, row_chunk)
            h = h_ref[rows, :]
            g = _dot(h, wg)
            u = _dot(h, wu)
            a = (0.5 * g * jax.nn.sigmoid(g) * u).astype(BF16)
            acc_cur[rows, :] += _dot(a, wd)

        @pl.when(f == pl.num_programs(1) - 1)
        def _():
            if emit_residual:
                @pl.when(has_prev)
                def _():
                    n_copy(i - 1).wait()

                n_buf[...] = _rms_norm(acc_cur[...], gout_ref[...]).astype(n_buf.dtype)
                n_copy(i).start()
            else:
                acc_cur[...] = _rms_norm(acc_cur[...], gout_ref[...])
            acc_out_copy(i, acc_cur).start()

            @pl.when(i == n_tiles - 1)
            def _():
                acc_out_copy(i, acc_cur).wait()
                if emit_residual:
                    n_copy(i).wait()

    parity = lax.rem(i, 2)

    @pl.when(parity == 0)
    def _():
        step(acc_a, acc_b)

    @pl.when(parity == 1)
    def _():
        step(acc_b, acc_a)


def _ffn(x, g_in, wg, wu, wd, g_out, *, emit_residual, name):
    t, d = x.shape
    d_ff = wg.shape[1]
    tm, tf = FFN_TOKEN_TILE, FFN_DFF_TILE
    assert d_ff // tf > FFN_X_PREFETCH_STEP
    hbm = pl.BlockSpec(memory_space=pl.ANY)
    vec = pl.BlockSpec((1, d), lambda i, f: (0, 0))
    n_shape = jax.ShapeDtypeStruct((t, d), BF16 if emit_residual else F32)
    return pl.pallas_call(
        functools.partial(_ffn_kernel, emit_residual=emit_residual, row_chunk=FFN_ROW_CHUNK),
        out_shape=(jax.ShapeDtypeStruct((t, d), F32), n_shape) if emit_residual else n_shape,
        grid=(t // tm, d_ff // tf),
        in_specs=[
            hbm, vec,
            pl.BlockSpec((d, tf), lambda i, f: (0, f)),
            pl.BlockSpec((d, tf), lambda i, f: (0, f)),
            pl.BlockSpec((tf, d), lambda i, f: (f, 0)),
            vec,
        ],
        out_specs=(hbm, hbm) if emit_residual else hbm,
        scratch_shapes=[pltpu.VMEM((tm, d), BF16), pltpu.VMEM((tm, d), F32), pltpu.VMEM((tm, d), F32)]
        + ([pltpu.VMEM((tm, d), BF16)] if emit_residual else []) + [pltpu.SemaphoreType.DMA((3,))],
        compiler_params=_params("arbitrary", "arbitrary"),
        name=name,
    )(x, g_in.reshape(1, d), wg, wu, wd, g_out.reshape(1, d))


def _in_proj_kernel(w_hbm, h_ref, bga_ref, bgc_ref, cw_ref, q_ref, k_ref, v_ref, cp_ref, ga_ref, gc_ref,
                    wstage_ref, wcat_ref, u_ref, sems, *, row_chunk):
    seq, tn = q_ref.shape
    d = h_ref.shape[1]
    j = pl.program_id(0)

    def w_copy(seg, tile_idx):
        col = pl.multiple_of(seg * d + tile_idx * tn, tn)
        return pltpu.make_async_copy(w_hbm.at[:, pl.ds(col, tn)], wstage_ref.at[seg], sems.at[seg])

    @pl.when(pl.program_id(1) == 0)
    def _():
        @pl.when(j == 0)
        def _():
            for seg in range(N_IN_SEGMENTS):
                w_copy(seg, 0).start()

        for seg in range(N_IN_SEGMENTS):
            w_copy(seg, j).wait()
            wcat_ref[:, seg * tn:(seg + 1) * tn] = wstage_ref[seg].astype(BF16)

        @pl.when(j + 1 < pl.num_programs(0))
        def _():
            for seg in range(N_IN_SEGMENTS):
                w_copy(seg, j + 1).start()

    halo = F32_SUBLANES
    u_ref[0:halo, :] = jnp.zeros((halo, tn), F32)
    cw = cw_ref[...]
    for r0 in range(0, seq, row_chunk):
        rows = pl.ds(r0, row_chunk)
        z = _dot(h_ref[rows, :], wcat_ref[...])
        zq, zk, zv, zb, zc, zx, zga, zgc = [z[:, seg * tn:(seg + 1) * tn] for seg in range(N_IN_SEGMENTS)]
        q_ref[rows, :] = (zq * QK_LOG2_SCALE).astype(BF16)
        k_ref[rows, :] = zk.astype(BF16)
        v_ref[rows, :] = zv.astype(BF16)
        ga_ref[rows, :] = jax.nn.sigmoid(zga + bga_ref[...]).astype(BF16)
        gc_ref[rows, :] = jax.nn.sigmoid(zgc + bgc_ref[...]).astype(BF16)
        u = zc * zx
        u_ref[pl.ds(halo + r0, row_chunk), :] = u
        conv = cw[2:3, :] * u
        for back in range(1, CONV_KERNEL):
            conv += cw[2 - back:3 - back, :] * u_ref[pl.ds(halo + r0 - back, row_chunk), :]
        cp_ref[rows, :] = (zb * conv).astype(BF16)


def _in_proj(hm, w_in, b_gate, conv_w, *, bsz, seq):
    t, d = hm.shape
    tn = IN_PROJ_CHANNEL_TILE
    nj = d // tn
    assert w_in.shape == (d, N_IN_SEGMENTS * d) and t == bsz * seq

    def b_spec(seg):
        return pl.BlockSpec((1, tn), lambda j, b, seg=seg: (0, seg * nj + j))

    tile = pl.BlockSpec((None, seq, tn), lambda j, b: (j, b, 0))
    out = jax.ShapeDtypeStruct((nj, t, tn), BF16)
    return pl.pallas_call(
        functools.partial(_in_proj_kernel, row_chunk=IN_PROJ_ROW_CHUNK),
        out_shape=(out, out, out, jax.ShapeDtypeStruct((t, d), BF16), out, out),
        grid=(nj, bsz),
        in_specs=[pl.BlockSpec(memory_space=pl.ANY), pl.BlockSpec((seq, d), lambda j, b: (b, 0)),
                  b_spec(0), b_spec(1), pl.BlockSpec((CONV_KERNEL, tn), lambda j, b: (0, j))],
        out_specs=(tile, tile, tile, pl.BlockSpec((seq, tn), lambda j, b: (b, j)), tile, tile),
        scratch_shapes=[pltpu.VMEM((N_IN_SEGMENTS, d, tn), F32),
                        pltpu.VMEM((d, N_IN_SEGMENTS * tn), BF16),
                        pltpu.VMEM((seq + F32_SUBLANES, tn), F32),
                        pltpu.SemaphoreType.DMA((N_IN_SEGMENTS,))],
        compiler_params=_params("arbitrary", "arbitrary"),
        name="in_proj",
    )(w_in, hm, b_gate.reshape(1, 2 * d), b_gate.reshape(1, 2 * d), conv_w)


def _moba_kernel(slope_ref, kaug_ref, q_ref, k_ref, v_ref, o_ref, qa_ref, ka_ref, va_ref):
    heads, seq, hd = q_ref.shape
    blk = MOBA_BLOCK
    nb = seq // blk

    @pl.when((pl.program_id(0) == 0) & (pl.program_id(1) == 0))
    def _():
        for hh in range(heads):
            ka_ref[hh, :, hd:] = kaug_ref[...]
            va_ref[hh, :, hd:] = jnp.ones((seq, hd), BF16)

    def gating(hh):
        q = q_ref[hh]
        k = k_ref[hh]
        kmean = jnp.mean(k.astype(F32).reshape(nb, blk, hd), axis=1)
        kmean_hi = kmean.astype(BF16)
        kmean_lo = (kmean - kmean_hi.astype(F32)).astype(BF16)
        gate2 = _dot_nt(jnp.concatenate([kmean_hi, kmean_lo], axis=0), q)
        gate = gate2[:nb] + gate2[nb:]
        blk_id = lax.broadcasted_iota(jnp.int32, (nb, seq), 0)
        q_blk = lax.broadcasted_iota(jnp.int32, (nb, seq), 1) // blk
        gate = jnp.where(blk_id < q_blk, gate, -jnp.inf)
        mask_rows = []
        for n in range(nb):
            g_n = gate[n:n + 1, :]
            beats = jnp.where(gate > g_n, 1.0, jnp.where((gate == g_n) & (blk_id < n), 1.0, 0.0))
            rank = jnp.sum(beats, axis=0, keepdims=True)
            q_blk_row = q_blk[0:1, :]
            visible = ((rank < float(MOBA_TOPK)) & (n < q_blk_row)) | (n == q_blk_row)
            mask_rows.append(jnp.where(visible, 0.0, MASK_NEG))
        head = pl.program_id(1) * heads + hh
        slope_rows = [jnp.full((1, seq), slope_ref[head * ALIBI_SLOPE_PARTS + part], F32)
                      for part in range(ALIBI_SLOPE_PARTS) for _ in range(ALIBI_POS_PARTS)]
        n_ext = nb + len(slope_rows)
        mask_t = jnp.concatenate(mask_rows + slope_rows + [jnp.zeros((hd - n_ext, seq), F32)], axis=0)
        qa_ref[hh, :, :hd] = q
        qa_ref[hh, :, hd:] = mask_t.T.astype(BF16)
        ka_ref[hh, :, :hd] = k
        va_ref[hh, :, :hd] = v_ref[hh]

    causal = (lax.broadcasted_iota(jnp.int32, (blk, blk), 0)
              >= lax.broadcasted_iota(jnp.int32, (blk, blk), 1))

    def scores(hh, qb):
        return _dot_nt(qa_ref[hh, qb * blk:(qb + 1) * blk, :], ka_ref[hh, 0:(qb + 1) * blk, :])

    def attention(hh, after_first_scores):
        order = list(range(nb - 1, -1, -1))
        s_next = scores(hh, order[0])
        after_first_scores()
        for idx, qb in enumerate(order):
            s = s_next
            if idx + 1 < nb:
                s_next = scores(hh, order[idx + 1])
            s_own = jnp.where(causal, s[:, qb * blk:], -jnp.inf)
            s = s_own if qb == 0 else jnp.concatenate([s[:, :qb * blk], s_own], axis=1)
            m = jnp.max(s, axis=-1, keepdims=True)
            p = jnp.exp2(s - m).astype(BF16)
            acc = _dot(p, va_ref[hh, 0:(qb + 1) * blk, :])
            o_ref[qb * blk:(qb + 1) * blk, hh * hd:(hh + 1) * hd] = (acc[:, :hd] / acc[:, hd:]).astype(o_ref.dtype)

    gating(0)
    for hh in range(heads):
        attention(hh, (lambda nxt=hh + 1: gating(nxt)) if hh + 1 < heads else (lambda: None))


def _alibi_slope_parts():
    rest = LOG2_E * jnp.exp2(-8.0 * jnp.arange(1, N_HEADS + 1, dtype=F32) / N_HEADS)
    parts = []
    for _ in range(ALIBI_SLOPE_PARTS):
        top = lax.bitcast_convert_type(
            lax.bitcast_convert_type(rest, jnp.uint32) & jnp.uint32(0xFFFF0000), F32)
        parts.append(top)
        rest = rest - top
    return jnp.stack(parts, axis=1).reshape(-1)


def _key_augmentation(seq):
    nb = seq // MOBA_BLOCK
    t_k = jnp.arange(seq, dtype=jnp.int32)[:, None]
    lane = lax.broadcasted_iota(jnp.int32, (1, HEAD_DIM), 1)
    pos_parts = ((t_k // MOBA_BLOCK) * MOBA_BLOCK, t_k % MOBA_BLOCK)
    assert len(pos_parts) == ALIBI_POS_PARTS and MOBA_BLOCK <= 256 and nb <= 256
    table = jnp.where(lane == t_k // MOBA_BLOCK, 1, 0)
    for col in range(ALIBI_SLOPE_PARTS * ALIBI_POS_PARTS):
        table = jnp.where(lane == nb + col, pos_parts[col % ALIBI_POS_PARTS], table)
    return table.astype(BF16)


def _moba(q, k, v, *, bsz, seq):
    n_heads, t, hd = q.shape
    assert (n_heads, hd) == (N_HEADS, HEAD_DIM) == (IN_PROJ_HEADS, IN_PROJ_CHANNEL_TILE)
    hps = MOBA_HEADS_PER_STEP
    tile = pl.BlockSpec((hps, seq, hd), lambda b, h, slopes: (h, b, 0))
    return pl.pallas_call(
        _moba_kernel,
        out_shape=jax.ShapeDtypeStruct((t, n_heads * hd), BF16),
        grid_spec=pltpu.PrefetchScalarGridSpec(
            num_scalar_prefetch=1,
            grid=(bsz, N_HEADS // hps),
            in_specs=[pl.BlockSpec((seq, hd), lambda b, h, slopes: (0, 0)), tile, tile, tile],
            out_specs=pl.BlockSpec((seq, hps * hd), lambda b, h, slopes: (b, h)),
            scratch_shapes=[pltpu.VMEM((hps, seq, 2 * hd), BF16)] * 3,
        ),
        compiler_params=_params("arbitrary", "arbitrary"),
        name="moba",
    )(_alibi_slope_parts(), _key_augmentation(seq), q, k, v)


def _mix_out_kernel(a_ref, c_ref, ga_ref, gc_ref, x_ref, wa_hbm, wc_hbm, wo_hbm, o_ref,
                    w_ref, stage_ref, sems, *, chunk):
    d = o_ref.shape[1]
    n_chunks = d // chunk
    weights = (wa_hbm, wc_hbm, wo_hbm)

    @pl.when(pl.program_id(0) == 0)
    def _():
        def w_copy(idx):
            w, k = divmod(idx, n_chunks)
            return pltpu.make_async_copy(weights[w].at[pl.ds(k * chunk, chunk), :],
                                         stage_ref.at[idx % 2], sems.at[idx % 2])

        w_copy(0).start()
        for idx in range(len(weights) * n_chunks):
            if idx + 1 < len(weights) * n_chunks:
                w_copy(idx + 1).start()
            w_copy(idx).wait()
            w, k = divmod(idx, n_chunks)
            w_ref[w, pl.ds(k * chunk, chunk), :] = stage_ref[idx % 2].astype(BF16)

    a = a_ref[...]
    c = c_ref[...]
    acc = x_ref[...]
    for c0 in range(0, d, chunk):
        cols = pl.ds(c0, chunk)
        ya = _dot(a, w_ref[0, :, cols])
        yc = _dot(c, w_ref[1, :, cols])
        tiles = range(c0 // ga_ref.shape[2], (c0 + chunk) // ga_ref.shape[2])
        ga = jnp.concatenate([ga_ref[t] for t in tiles], axis=1).astype(F32)
        gc = jnp.concatenate([gc_ref[t] for t in tiles], axis=1).astype(F32)
        m = (ga * ya + gc * yc).astype(BF16)
        acc = acc + _dot(m, w_ref[2, cols, :])
    o_ref[...] = acc


def _mix_out(attn, conv, wa, wc, ga, gc, wo, x1):
    t, d = attn.shape
    tm, chunk = MIX_TOKEN_TILE, MIX_CHANNEL_CHUNK
    row = pl.BlockSpec((tm, d), lambda i: (i, 0))
    gate = pl.BlockSpec((ga.shape[0], tm, ga.shape[2]), lambda i: (0, i, 0))
    hbm = pl.BlockSpec(memory_space=pl.ANY)
    return pl.pallas_call(
        functools.partial(_mix_out_kernel, chunk=chunk),
        out_shape=jax.ShapeDtypeStruct((t, d), F32),
        grid=(t // tm,),
        in_specs=[row, row, gate, gate, row, hbm, hbm, hbm],
        out_specs=row,
        scratch_shapes=[pltpu.VMEM((3, d, d), BF16), pltpu.VMEM((2, chunk, d), F32),
                        pltpu.SemaphoreType.DMA((2,))],
        compiler_params=_params("arbitrary"),
        name="mix_out",
    )(attn, conv, ga, gc, x1, wa, wc, wo)


def kernel(x, ffn1_norm, ffn1_w_gate, ffn1_w_up, ffn1_w_down, mix_norm, w_in, b_gate, conv_w,
           w_attn_out, w_conv_out, w_out, ffn2_norm, ffn2_w_gate, ffn2_w_up, ffn2_w_down, final_norm):
    bsz, seq, d = x.shape
    depth = ffn1_norm.shape[0]
    assert d == N_HEADS * HEAD_DIM and seq % MOBA_BLOCK == 0 and seq // MOBA_BLOCK <= HEAD_DIM
    xs = x.reshape(bsz * seq, d)
    for l in range(depth):
        last = l == depth - 1
        x1, hm = _ffn(xs, ffn1_norm[l], ffn1_w_gate[l], ffn1_w_up[l], ffn1_w_down[l], mix_norm[l],
                      emit_residual=True, name="ffn1")
        q, k, v, conv, ga, gc = _in_proj(hm, w_in[l], b_gate[l], conv_w[l], bsz=bsz, seq=seq)
        attn = _moba(q, k, v, bsz=bsz, seq=seq)
        x2 = _mix_out(attn, conv, w_attn_out[l], w_conv_out[l], ga, gc, w_out[l], x1)
        if last:
            xs = _ffn(x2, ffn2_norm[l], ffn2_w_gate[l], ffn2_w_up[l], ffn2_w_down[l], final_norm,
                      emit_residual=False, name="ffn2")
        else:
            one = jnp.ones((d,), F32)
            xs, _ = _ffn(x2, ffn2_norm[l], ffn2_w_gate[l], ffn2_w_up[l], ffn2_w_down[l], one,
                         emit_residual=True, name="ffn2")
    return xs.reshape(bsz, seq, d)
```

```python
import functools

import jax
import jax.numpy as jnp
from jax import lax
from jax.experimental import pallas as pl
from jax.experimental.pallas import tpu as pltpu

F32 = jnp.float32
BF16 = jnp.bfloat16

N_HEADS = 16
HEAD_DIM = 128
CONV_KERNEL = 3
MOBA_BLOCK = 256
MOBA_TOPK = 3
MOBA_HEADS_PER_STEP = 4
ALIBI_SLOPE_PARTS = 3
ALIBI_POS_PARTS = 2
RMS_EPS = 1e-6
N_IN_SEGMENTS = 8

F32_SUBLANES = 8
V7X_VMEM_BYTES = 64 * 1024 * 1024
VMEM_LIMIT_BYTES = V7X_VMEM_BYTES - 8 * 1024 * 1024

FFN_TOKEN_TILE = 1024
FFN_DFF_TILE = 512
FFN_ROW_CHUNK = 512
FFN_X_PREFETCH_STEP = 2

IN_PROJ_CHANNEL_TILE = 128
IN_PROJ_HEADS = N_HEADS * HEAD_DIM // IN_PROJ_CHANNEL_TILE
IN_PROJ_ROW_CHUNK = 512

MIX_TOKEN_TILE = 256
MIX_CHANNEL_CHUNK = 512

LOG2_E = 1.4426950408889634
QK_LOG2_SCALE = LOG2_E * HEAD_DIM ** -0.5

MASK_NEG = -1e30


def _params(*semantics):
    return pltpu.CompilerParams(dimension_semantics=semantics, vmem_limit_bytes=VMEM_LIMIT_BYTES)


def _rms_norm(x, g):
    return x * lax.rsqrt(jnp.mean(x * x, axis=-1, keepdims=True) + RMS_EPS) * g


def _dot(a, b):
    return jnp.dot(a, b, preferred_element_type=F32)


def _dot_nt(a, b):
    return lax.dot_general(a, b, (((1,), (1,)), ((), ())), preferred_element_type=F32)


def _ffn_kernel(x_hbm, gin_ref, wg_ref, wu_ref, wd_ref, gout_ref, *rest, emit_residual, row_chunk):
    if emit_residual:
        y_hbm, n_hbm, h_ref, acc_a, acc_b, n_buf, sems = rest
    else:
        n_hbm, h_ref, acc_a, acc_b, sems = rest
        y_hbm = n_buf = None
    tm = h_ref.shape[0]
    i = pl.program_id(0)
    f = pl.program_id(1)
    n_tiles = pl.num_programs(0)
    has_prev = i > 0
    has_next = i + 1 < n_tiles

    def hbm_tile(ref, t):
        return ref.at[pl.ds(pl.multiple_of(t * tm, tm), tm), :]

    def x_copy(t, acc):
        return pltpu.make_async_copy(hbm_tile(x_hbm, t), acc, sems.at[0])

    def acc_out_copy(t, acc):
        return pltpu.make_async_copy(acc, hbm_tile(y_hbm if emit_residual else n_hbm, t), sems.at[1])

    def n_copy(t):
        return pltpu.make_async_copy(n_buf, hbm_tile(n_hbm, t), sems.at[2])

    def step(acc_cur, acc_oth):
        @pl.when(f == 0)
        def _():
            @pl.when(i == 0)
            def _():
                x_copy(0, acc_cur).start()

            x_copy(i, acc_cur).wait()
            h_ref[...] = _rms_norm(acc_cur[...], gin_ref[...]).astype(BF16)

        @pl.when(f == FFN_X_PREFETCH_STEP)
        def _():
            @pl.when(has_prev)
            def _():
                acc_out_copy(i - 1, acc_oth).wait()

            @pl.when(has_next)
            def _():
                x_copy(i + 1, acc_oth).start()

        wg = wg_ref[...].astype(BF16)
        wu = wu_ref[...].astype(BF16)
        wd = wd_ref[...].astype(BF16)
        for r0 in range(0, tm, row_chunk):
            rows = pl.ds(r0, row_chunk)
            h = h_ref[rows, :]
            g = _dot(h, wg)
            u = _dot(h, wu)
            a = (0.5 * g * jax.nn.sigmoid(g) * u).astype(BF16)
            acc_cur[rows, :] += _dot(a, wd)

        @pl.when(f == pl.num_programs(1) - 1)
        def _():
            if emit_residual:
                @pl.when(has_prev)
                def _():
                    n_copy(i - 1).wait()

                n_buf[...] = _rms_norm(acc_cur[...], gout_ref[...]).astype(n_buf.dtype)
                n_copy(i).start()
            else:
                acc_cur[...] = _rms_norm(acc_cur[...], gout_ref[...])
            acc_out_copy(i, acc_cur).start()

            @pl.when(i == n_tiles - 1)
            def _():
                acc_out_copy(i, acc_cur).wait()
                if emit_residual:
                    n_copy(i).wait()

    parity = lax.rem(i, 2)

    @pl.when(parity == 0)
    def _():
        step(acc_a, acc_b)

    @pl.when(parity == 1)
    def _():
        step(acc_b, acc_a)


def _ffn(x, g_in, wg, wu, wd, g_out, *, emit_residual, name):
    t, d = x.shape
    d_ff = wg.shape[1]
    tm, tf = FFN_TOKEN_TILE, FFN_DFF_TILE
    assert d_ff // tf > FFN_X_PREFETCH_STEP
    hbm = pl.BlockSpec(memory_space=pl.ANY)
    vec = pl.BlockSpec((1, d), lambda i, f: (0, 0))
    n_shape = jax.ShapeDtypeStruct((t, d), BF16 if emit_residual else F32)
    return pl.pallas_call(
        functools.partial(_ffn_kernel, emit_residual=emit_residual, row_chunk=FFN_ROW_CHUNK),
        out_shape=(jax.ShapeDtypeStruct((t, d), F32), n_shape) if emit_residual else n_shape,
        grid=(t // tm, d_ff // tf),
        in_specs=[
            hbm, vec,
            pl.BlockSpec((d, tf), lambda i, f: (0, f)),
            pl.BlockSpec((d, tf), lambda i, f: (0, f)),
            pl.BlockSpec((tf, d), lambda i, f: (f, 0)),
            vec,
        ],
        out_specs=(hbm, hbm) if emit_residual else hbm,
        scratch_shapes=[pltpu.VMEM((tm, d), BF16), pltpu.VMEM((tm, d), F32), pltpu.VMEM((tm, d), F32)]
        + ([pltpu.VMEM((tm, d), BF16)] if emit_residual else []) + [pltpu.SemaphoreType.DMA((3,))],
        compiler_params=_params("arbitrary", "arbitrary"),
        name=name,
    )(x, g_in.reshape(1, d), wg, wu, wd, g_out.reshape(1, d))


def _in_proj_kernel(w_hbm, h_ref, bga_ref, bgc_ref, cw_ref, q_ref, k_ref, v_ref, cp_ref, ga_ref, gc_ref,
                    wstage_ref, wcat_ref, u_ref, sems, *, row_chunk):
    seq, tn = q_ref.shape
    d = h_ref.shape[1]
    j = pl.program_id(0)

    def w_copy(seg, tile_idx):
        col = pl.multiple_of(seg * d + tile_idx * tn, tn)
        return pltpu.make_async_copy(w_hbm.at[:, pl.ds(col, tn)], wstage_ref.at[seg], sems.at[seg])

    @pl.when(pl.program_id(1) == 0)
    def _():
        @pl.when(j == 0)
        def _():
            for seg in range(N_IN_SEGMENTS):
                w_copy(seg, 0).start()

        for seg in range(N_IN_SEGMENTS):
            w_copy(seg, j).wait()
            wcat_ref[:, seg * tn:(seg + 1) * tn] = wstage_ref[seg].astype(BF16)

        @pl.when(j + 1 < pl.num_programs(0))
        def _():
            for seg in range(N_IN_SEGMENTS):
                w_copy(seg, j + 1).start()

    halo = F32_SUBLANES
    u_ref[0:halo, :] = jnp.zeros((halo, tn), F32)
    cw = cw_ref[...]
    for r0 in range(0, seq, row_chunk):
        rows = pl.ds(r0, row_chunk)
        z = _dot(h_ref[rows, :], wcat_ref[...])
        zq, zk, zv, zb, zc, zx, zga, zgc = [z[:, seg * tn:(seg + 1) * tn] for seg in range(N_IN_SEGMENTS)]
        q_ref[rows, :] = (zq * QK_LOG2_SCALE).astype(BF16)
        k_ref[rows, :] = zk.astype(BF16)
        v_ref[rows, :] = zv.astype(BF16)
        ga_ref[rows, :] = jax.nn.sigmoid(zga + bga_ref[...]).astype(BF16)
        gc_ref[rows, :] = jax.nn.sigmoid(zgc + bgc_ref[...]).astype(BF16)
        u = zc * zx
        u_ref[pl.ds(halo + r0, row_chunk), :] = u
        conv = cw[2:3, :] * u
        for back in range(1, CONV_KERNEL):
            conv += cw[2 - back:3 - back, :] * u_ref[pl.ds(halo + r0 - back, row_chunk), :]
        cp_ref[rows, :] = (zb * conv).astype(BF16)


def _in_proj(hm, w_in, b_gate, conv_w, *, bsz, seq):
    t, d = hm.shape
    tn = IN_PROJ_CHANNEL_TILE
    nj = d // tn
    assert w_in.shape == (d, N_IN_SEGMENTS * d) and t == bsz * seq

    def b_spec(seg):
        return pl.BlockSpec((1, tn), lambda j, b, seg=seg: (0, seg * nj + j))

    tile = pl.BlockSpec((None, seq, tn), lambda j, b: (j, b, 0))
    out = jax.ShapeDtypeStruct((nj, t, tn), BF16)
    return pl.pallas_call(
        functools.partial(_in_proj_kernel, row_chunk=IN_PROJ_ROW_CHUNK),
        out_shape=(out, out, out, jax.ShapeDtypeStruct((t, d), BF16), out, out),
        grid=(nj, bsz),
        in_specs=[pl.BlockSpec(memory_space=pl.ANY), pl.BlockSpec((seq, d), lambda j, b: (b, 0)),
                  b_spec(0), b_spec(1), pl.BlockSpec((CONV_KERNEL, tn), lambda j, b: (0, j))],
        out_specs=(tile, tile, tile, pl.BlockSpec((seq, tn), lambda j, b: (b, j)), tile, tile),
        scratch_shapes=[pltpu.VMEM((N_IN_SEGMENTS, d, tn), F32),
                        pltpu.VMEM((d, N_IN_SEGMENTS * tn), BF16),
                        pltpu.VMEM((seq + F32_SUBLANES, tn), F32),
                        pltpu.SemaphoreType.DMA((N_IN_SEGMENTS,))],
        compiler_params=_params("arbitrary", "arbitrary"),
        name="in_proj",
    )(w_in, hm, b_gate.reshape(1, 2 * d), b_gate.reshape(1, 2 * d), conv_w)


def _moba_kernel(slope_ref, kaug_ref, q_ref, k_ref, v_ref, o_ref, qa_ref, ka_ref, va_ref):
    heads, seq, hd = q_ref.shape
    blk = MOBA_BLOCK
    nb = seq // blk

    @pl.when((pl.program_id(0) == 0) & (pl.program_id(1) == 0))
    def _():
        for hh in range(heads):
            ka_ref[hh, :, hd:] = kaug_ref[...]
            va_ref[hh, :, hd:] = jnp.ones((seq, hd), BF16)

    def gating(hh):
        q = q_ref[hh]
        k = k_ref[hh]
        kmean = jnp.mean(k.astype(F32).reshape(nb, blk, hd), axis=1)
        kmean_hi = kmean.astype(BF16)
        kmean_lo = (kmean - kmean_hi.astype(F32)).astype(BF16)
        gate2 = _dot_nt(jnp.concatenate([kmean_hi, kmean_lo], axis=0), q)
        gate = gate2[:nb] + gate2[nb:]
        blk_id = lax.broadcasted_iota(jnp.int32, (nb, seq), 0)
        q_blk = lax.broadcasted_iota(jnp.int32, (nb, seq), 1) // blk
        gate = jnp.where(blk_id < q_blk, gate, -jnp.inf)
        mask_rows = []
        for n in range(nb):
            g_n = gate[n:n + 1, :]
            beats = jnp.where(gate > g_n, 1.0, jnp.where((gate == g_n) & (blk_id < n), 1.0, 0.0))
            rank = jnp.sum(beats, axis=0, keepdims=True)
            q_blk_row = q_blk[0:1, :]
            visible = ((rank < float(MOBA_TOPK)) & (n < q_blk_row)) | (n == q_blk_row)
            mask_rows.append(jnp.where(visible, 0.0, MASK_NEG))
        head = pl.program_id(1) * heads + hh
        slope_rows = [jnp.full((1, seq), slope_ref[head * ALIBI_SLOPE_PARTS + part], F32)
                      for part in range(ALIBI_SLOPE_PARTS) for _ in range(ALIBI_POS_PARTS)]
        n_ext = nb + len(slope_rows)
        mask_t = jnp.concatenate(mask_rows + slope_rows + [jnp.zeros((hd - n_ext, seq), F32)], axis=0)
        qa_ref[hh, :, :hd] = q
        qa_ref[hh, :, hd:] = mask_t.T.astype(BF16)
        ka_ref[hh, :, :hd] = k
        va_ref[hh, :, :hd] = v_ref[hh]

    causal = (lax.broadcasted_iota(jnp.int32, (blk, blk), 0)
              >= lax.broadcasted_iota(jnp.int32, (blk, blk), 1))

    def scores(hh, qb):
        return _dot_nt(qa_ref[hh, qb * blk:(qb + 1) * blk, :], ka_ref[hh, 0:(qb + 1) * blk, :])

    def attention(hh, after_first_scores):
        order = list(range(nb - 1, -1, -1))
        s_next = scores(hh, order[0])
        after_first_scores()
        for idx, qb in enumerate(order):
            s = s_next
            if idx + 1 < nb:
                s_next = scores(hh, order[idx + 1])
            s_own = jnp.where(causal, s[:, qb * blk:], -jnp.inf)
            s = s_own if qb == 0 else jnp.concatenate([s[:, :qb * blk], s_own], axis=1)
            m = jnp.max(s, axis=-1, keepdims=True)
            p = jnp.exp2(s - m).astype(BF16)
            acc = _dot(p, va_ref[hh, 0:(qb + 1) * blk, :])
            o_ref[qb * blk:(qb + 1) * blk, hh * hd:(hh + 1) * hd] = (acc[:, :hd] / acc[:, hd:]).astype(o_ref.dtype)

    gating(0)
    for hh in range(heads):
        attention(hh, (lambda nxt=hh + 1: gating(nxt)) if hh + 1 < heads else (lambda: None))


def _alibi_slope_parts():
    rest = LOG2_E * jnp.exp2(-8.0 * jnp.arange(1, N_HEADS + 1, dtype=F32) / N_HEADS)
    parts = []
    for _ in range(ALIBI_SLOPE_PARTS):
        top = lax.bitcast_convert_type(
            lax.bitcast_convert_type(rest, jnp.uint32) & jnp.uint32(0xFFFF0000), F32)
        parts.append(top)
        rest = rest - top
    return jnp.stack(parts, axis=1).reshape(-1)


def _key_augmentation(seq):
    nb = seq // MOBA_BLOCK
    t_k = jnp.arange(seq, dtype=jnp.int32)[:, None]
    lane = lax.broadcasted_iota(jnp.int32, (1, HEAD_DIM), 1)
    pos_parts = ((t_k // MOBA_BLOCK) * MOBA_BLOCK, t_k % MOBA_BLOCK)
    assert len(pos_parts) == ALIBI_POS_PARTS and MOBA_BLOCK <= 256 and nb <= 256
    table = jnp.where(lane == t_k // MOBA_BLOCK, 1, 0)
    for col in range(ALIBI_SLOPE_PARTS * ALIBI_POS_PARTS):
        table = jnp.where(lane == nb + col, pos_parts[col % ALIBI_POS_PARTS], table)
    return table.astype(BF16)


def _moba(q, k, v, *, bsz, seq):
    n_heads, t, hd = q.shape
    assert (n_heads, hd) == (N_HEADS, HEAD_DIM) == (IN_PROJ_HEADS, IN_PROJ_CHANNEL_TILE)
    hps = MOBA_HEADS_PER_STEP
    tile = pl.BlockSpec((hps, seq, hd), lambda b, h, slopes: (h, b, 0))
    return pl.pallas_call(
        _moba_kernel,
        out_shape=jax.ShapeDtypeStruct((t, n_heads * hd), BF16),
        grid_spec=pltpu.PrefetchScalarGridSpec(
            num_scalar_prefetch=1,
            grid=(bsz, N_HEADS // hps),
            in_specs=[pl.BlockSpec((seq, hd), lambda b, h, slopes: (0, 0)), tile, tile, tile],
            out_specs=pl.BlockSpec((seq, hps * hd), lambda b, h, slopes: (b, h)),
            scratch_shapes=[pltpu.VMEM((hps, seq, 2 * hd), BF16)] * 3,
        ),
        compiler_params=_params("arbitrary", "arbitrary"),
        name="moba",
    )(_alibi_slope_parts(), _key_augmentation(seq), q, k, v)


def _mix_out_kernel(a_ref, c_ref, ga_ref, gc_ref, x_ref, wa_hbm, wc_hbm, wo_hbm, o_ref,
                    w_ref, stage_ref, sems, *, chunk):
    d = o_ref.shape[1]
    n_chunks = d // chunk
    weights = (wa_hbm, wc_hbm, wo_hbm)

    @pl.when(pl.program_id(0) == 0)
    def _():
        def w_copy(idx):
            w, k = divmod(idx, n_chunks)
            return pltpu.make_async_copy(weights[w].at[pl.ds(k * chunk, chunk), :],
                                         stage_ref.at[idx % 2], sems.at[idx % 2])

        w_copy(0).start()
        for idx in range(len(weights) * n_chunks):
            if idx + 1 < len(weights) * n_chunks:
                w_copy(idx + 1).start()
            w_copy(idx).wait()
            w, k = divmod(idx, n_chunks)
            w_ref[w, pl.ds(k * chunk, chunk), :] = stage_ref[idx % 2].astype(BF16)

    a = a_ref[...]
    c = c_ref[...]
    acc = x_ref[...]
    for c0 in range(0, d, chunk):
        cols = pl.ds(c0, chunk)
        ya = _dot(a, w_ref[0, :, cols])
        yc = _dot(c, w_ref[1, :, cols])
        tiles = range(c0 // ga_ref.shape[2], (c0 + chunk) // ga_ref.shape[2])
        ga = jnp.concatenate([ga_ref[t] for t in tiles], axis=1).astype(F32)
        gc = jnp.concatenate([gc_ref[t] for t in tiles], axis=1).astype(F32)
        m = (ga * ya + gc * yc).astype(BF16)
        acc = acc + _dot(m, w_ref[2, cols, :])
    o_ref[...] = acc


def _mix_out(attn, conv, wa, wc, ga, gc, wo, x1):
    t, d = attn.shape
    tm, chunk = MIX_TOKEN_TILE, MIX_CHANNEL_CHUNK
    row = pl.BlockSpec((tm, d), lambda i: (i, 0))
    gate = pl.BlockSpec((ga.shape[0], tm, ga.shape[2]), lambda i: (0, i, 0))
    hbm = pl.BlockSpec(memory_space=pl.ANY)
    return pl.pallas_call(
        functools.partial(_mix_out_kernel, chunk=chunk),
        out_shape=jax.ShapeDtypeStruct((t, d), F32),
        grid=(t // tm,),
        in_specs=[row, row, gate, gate, row, hbm, hbm, hbm],
        out_specs=row,
        scratch_shapes=[pltpu.VMEM((3, d, d), BF16), pltpu.VMEM((2, chunk, d), F32),
                        pltpu.SemaphoreType.DMA((2,))],
        compiler_params=_params("arbitrary"),
        name="mix_out",
    )(attn, conv, ga, gc, x1, wa, wc, wo)


def kernel(x, ffn1_norm, ffn1_w_gate, ffn1_w_up, ffn1_w_down, mix_norm, w_in, b_gate, conv_w,
           w_attn_out, w_conv_out, w_out, ffn2_norm, ffn2_w_gate, ffn2_w_up, ffn2_w_down, final_norm):
    bsz, seq, d = x.shape
    depth = ffn1_norm.shape[0]
    assert d == N_HEADS * HEAD_DIM and seq % MOBA_BLOCK == 0 and seq // MOBA_BLOCK <= HEAD_DIM
    xs = x.reshape(bsz * seq, d)
    for l in range(depth):
        last = l == depth - 1
        x1, hm = _ffn(xs, ffn1_norm[l], ffn1_w_gate[l], ffn1_w_up[l], ffn1_w_down[l], mix_norm[l],
                      emit_residual=True, name="ffn1")
        q, k, v, conv, ga, gc = _in_proj(hm, w_in[l], b_gate[l], conv_w[l], bsz=bsz, seq=seq)
        attn = _moba(q, k, v, bsz=bsz, seq=seq)
        x2 = _mix_out(attn, conv, w_attn_out[l], w_conv_out[l], ga, gc, w_out[l], x1)
        if last:
            xs = _ffn(x2, ffn2_norm[l], ffn2_w_gate[l], ffn2_w_up[l], ffn2_w_down[l], final_norm,
                      emit_residual=False, name="ffn2")
        else:
            one = jnp.ones((d,), F32)
            xs, _ = _ffn(x2, ffn2_norm[l], ffn2_w_gate[l], ffn2_w_up[l], ffn2_w_down[l], one,
                         emit_residual=True, name="ffn2")
    return xs.reshape(bsz, seq, d)
```

```python
import functools

import jax
import jax.numpy as jnp
from jax import lax
from jax.experimental import pallas as pl
from jax.experimental.pallas import tpu as pltpu

F32 = jnp.float32
BF16 = jnp.bfloat16

N_HEADS = 16
HEAD_DIM = 128
CONV_KERNEL = 3
MOBA_BLOCK = 256
MOBA_TOPK = 3
MOBA_HEADS_PER_STEP = 4
ALIBI_SLOPE_PARTS = 3
ALIBI_POS_PARTS = 2
RMS_EPS = 1e-6
N_IN_SEGMENTS = 8

F32_SUBLANES = 8
V7X_VMEM_BYTES = 64 * 1024 * 1024
VMEM_LIMIT_BYTES = V7X_VMEM_BYTES - 8 * 1024 * 1024

FFN_TOKEN_TILE = 1024
FFN_DFF_TILE = 512
FFN_ROW_CHUNK = 512
FFN_X_PREFETCH_STEP = 2

IN_PROJ_CHANNEL_TILE = 128
IN_PROJ_HEADS = N_HEADS * HEAD_DIM // IN_PROJ_CHANNEL_TILE
IN_PROJ_ROW_CHUNK = 512

MIX_TOKEN_TILE = 256
MIX_CHANNEL_CHUNK = 512

LOG2_E = 1.4426950408889634
QK_LOG2_SCALE = LOG2_E * HEAD_DIM ** -0.5

MASK_NEG = -1e30


def _params(*semantics):
    return pltpu.CompilerParams(dimension_semantics=semantics, vmem_limit_bytes=VMEM_LIMIT_BYTES)


def _rms_norm(x, g):
    return x * lax.rsqrt(jnp.mean(x * x, axis=-1, keepdims=True) + RMS_EPS) * g


def _dot(a, b):
    return jnp.dot(a, b, preferred_element_type=F32)


def _dot_nt(a, b):
    return lax.dot_general(a, b, (((1,), (1,)), ((), ())), preferred_element_type=F32)


def _ffn_kernel(x_hbm, gin_ref, wg_ref, wu_ref, wd_ref, gout_ref, *rest, emit_residual, row_chunk):
    if emit_residual:
        y_hbm, n_hbm, h_ref, acc_a, acc_b, n_buf, sems = rest
    else:
        n_hbm, h_ref, acc_a, acc_b, sems = rest
        y_hbm = n_buf = None
    tm = h_ref.shape[0]
    i = pl.program_id(0)
    f = pl.program_id(1)
    n_tiles = pl.num_programs(0)
    has_prev = i > 0
    has_next = i + 1 < n_tiles

    def hbm_tile(ref, t):
        return ref.at[pl.ds(pl.multiple_of(t * tm, tm), tm), :]

    def x_copy(t, acc):
        return pltpu.make_async_copy(hbm_tile(x_hbm, t), acc, sems.at[0])

    def acc_out_copy(t, acc):
        return pltpu.make_async_copy(acc, hbm_tile(y_hbm if emit_residual else n_hbm, t), sems.at[1])

    def n_copy(t):
        return pltpu.make_async_copy(n_buf, hbm_tile(n_hbm, t), sems.at[2])

    def step(acc_cur, acc_oth):
        @pl.when(f == 0)
        def _():
            @pl.when(i == 0)
            def _():
                x_copy(0, acc_cur).start()

            x_copy(i, acc_cur).wait()
            h_ref[...] = _rms_norm(acc_cur[...], gin_ref[...]).astype(BF16)

        @pl.when(f == FFN_X_PREFETCH_STEP)
        def _():
            @pl.when(has_prev)
            def _():
                acc_out_copy(i - 1, acc_oth).wait()

            @pl.when(has_next)
            def _():
                x_copy(i + 1, acc_oth).start()

        wg = wg_ref[...].astype(BF16)
        wu = wu_ref[...].astype(BF16)
        wd = wd_ref[...].astype(BF16)
        for r0 in range(0, tm, row_chunk):
            rows = pl.ds(r0, row_chunk)
            h = h_ref[rows, :]
            g = _dot(h, wg)
            u = _dot(h, wu)
            a = (0.5 * g * jax.nn.sigmoid(g) * u).astype(BF16)
            acc_cur[rows, :] += _dot(a, wd)

        @pl.when(f == pl.num_programs(1) - 1)
        def _():
            if emit_residual:
                @pl.when(has_prev)
                def _():
                    n_copy(i - 1).wait()

                n_buf[...] = _rms_norm(acc_cur[...], gout_ref[...]).astype(n_buf.dtype)
                n_copy(i).start()
            else:
                acc_cur[...] = _rms_norm(acc_cur[...], gout_ref[...])
            acc_out_copy(i, acc_cur).start()

            @pl.when(i == n_tiles - 1)
            def _():
                acc_out_copy(i, acc_cur).wait()
                if emit_residual:
                    n_copy(i).wait()

    parity = lax.rem(i, 2)

    @pl.when(parity == 0)
    def _():
        step(acc_a, acc_b)

    @pl.when(parity == 1)
    def _():
        step(acc_b, acc_a)


def _ffn(x, g_in, wg, wu, wd, g_out, *, emit_residual, name):
    t, d = x.shape
    d_ff = wg.shape[1]
    tm, tf = FFN_TOKEN_TILE, FFN_DFF_TILE
    assert d_ff // tf > FFN_X_PREFETCH_STEP
    hbm = pl.BlockSpec(memory_space=pl.ANY)
    vec = pl.BlockSpec((1, d), lambda i, f: (0, 0))
    n_shape = jax.ShapeDtypeStruct((t, d), BF16 if emit_residual else F32)
    return pl.pallas_call(
        functools.partial(_ffn_kernel, emit_residual=emit_residual, row_chunk=FFN_ROW_CHUNK),
        out_shape=(jax.ShapeDtypeStruct((t, d), F32), n_shape) if emit_residual else n_shape,
        grid=(t // tm, d_ff // tf),
        in_specs=[
            hbm, vec,
            pl.BlockSpec((d, tf), lambda i, f: (0, f)),
            pl.BlockSpec((d, tf), lambda i, f: (0, f)),
            pl.BlockSpec((tf, d), lambda i, f: (f, 0)),
            vec,
        ],
        out_specs=(hbm, hbm) if emit_residual else hbm,
        scratch_shapes=[pltpu.VMEM((tm, d), BF16), pltpu.VMEM((tm, d), F32), pltpu.VMEM((tm, d), F32)]
        + ([pltpu.VMEM((tm, d), BF16)] if emit_residual else []) + [pltpu.SemaphoreType.DMA((3,))],
        compiler_params=_params("arbitrary", "arbitrary"),
        name=name,
    )(x, g_in.reshape(1, d), wg, wu, wd, g_out.reshape(1, d))


def _in_proj_kernel(w_hbm, h_ref, bga_ref, bgc_ref, cw_ref, q_ref, k_ref, v_ref, cp_ref, ga_ref, gc_ref,
                    wstage_ref, wcat_ref, u_ref, sems, *, row_chunk):
    seq, tn = q_ref.shape
    d = h_ref.shape[1]
    j = pl.program_id(0)

    def w_copy(seg, tile_idx):
        col = pl.multiple_of(seg * d + tile_idx * tn, tn)
        return pltpu.make_async_copy(w_hbm.at[:, pl.ds(col, tn)], wstage_ref.at[seg], sems.at[seg])

    @pl.when(pl.program_id(1) == 0)
    def _():
        @pl.when(j == 0)
        def _():
            for seg in range(N_IN_SEGMENTS):
                w_copy(seg, 0).start()

        for seg in range(N_IN_SEGMENTS):
            w_copy(seg, j).wait()
            wcat_ref[:, seg * tn:(seg + 1) * tn] = wstage_ref[seg].astype(BF16)

        @pl.when(j + 1 < pl.num_programs(0))
        def _():
            for seg in range(N_IN_SEGMENTS):
                w_copy(seg, j + 1).start()

    halo = F32_SUBLANES
    u_ref[0:halo, :] = jnp.zeros((halo, tn), F32)
    cw = cw_ref[...]
    for r0 in range(0, seq, row_chunk):
        rows = pl.ds(r0, row_chunk)
        z = _dot(h_ref[rows, :], wcat_ref[...])
        zq, zk, zv, zb, zc, zx, zga, zgc = [z[:, seg * tn:(seg + 1) * tn] for seg in range(N_IN_SEGMENTS)]
        q_ref[rows, :] = (zq * QK_LOG2_SCALE).astype(BF16)
        k_ref[rows, :] = zk.astype(BF16)
        v_ref[rows, :] = zv.astype(BF16)
        ga_ref[rows, :] = jax.nn.sigmoid(zga + bga_ref[...]).astype(BF16)
        gc_ref[rows, :] = jax.nn.sigmoid(zgc + bgc_ref[...]).astype(BF16)
        u = zc * zx
        u_ref[pl.ds(halo + r0, row_chunk), :] = u
        conv = cw[2:3, :] * u
        for back in range(1, CONV_KERNEL):
            conv += cw[2 - back:3 - back, :] * u_ref[pl.ds(halo + r0 - back, row_chunk), :]
        cp_ref[rows, :] = (zb * conv).astype(BF16)


def _in_proj(hm, w_in, b_gate, conv_w, *, bsz, seq):
    t, d = hm.shape
    tn = IN_PROJ_CHANNEL_TILE
    nj = d // tn
    assert w_in.shape == (d, N_IN_SEGMENTS * d) and t == bsz * seq

    def b_spec(seg):
        return pl.BlockSpec((1, tn), lambda j, b, seg=seg: (0, seg * nj + j))

    tile = pl.BlockSpec((None, seq, tn), lambda j, b: (j, b, 0))
    out = jax.ShapeDtypeStruct((nj, t, tn), BF16)
    return pl.pallas_call(
        functools.partial(_in_proj_kernel, row_chunk=IN_PROJ_ROW_CHUNK),
        out_shape=(out, out, out, jax.ShapeDtypeStruct((t, d), BF16), out, out),
        grid=(nj, bsz),
        in_specs=[pl.BlockSpec(memory_space=pl.ANY), pl.BlockSpec((seq, d), lambda j, b: (b, 0)),
                  b_spec(0), b_spec(1), pl.BlockSpec((CONV_KERNEL, tn), lambda j, b: (0, j))],
        out_specs=(tile, tile, tile, pl.BlockSpec((seq, tn), lambda j, b: (b, j)), tile, tile),
        scratch_shapes=[pltpu.VMEM((N_IN_SEGMENTS, d, tn), F32),
                        pltpu.VMEM((d, N_IN_SEGMENTS * tn), BF16),
                        pltpu.VMEM((seq + F32_SUBLANES, tn), F32),
                        pltpu.SemaphoreType.DMA((N_IN_SEGMENTS,))],
        compiler_params=_params("arbitrary", "arbitrary"),
        name="in_proj",
    )(w_in, hm, b_gate.reshape(1, 2 * d), b_gate.reshape(1, 2 * d), conv_w)


def _moba_kernel(slope_ref, kaug_ref, q_ref, k_ref, v_ref, o_ref, qa_ref, ka_ref, va_ref):
    heads, seq, hd = q_ref.shape
    blk = MOBA_BLOCK
    nb = seq // blk

    @pl.when((pl.program_id(0) == 0) & (pl.program_id(1) == 0))
    def _():
        for hh in range(heads):
            ka_ref[hh, :, hd:] = kaug_ref[...]
            va_ref[hh, :, hd:] = jnp.ones((seq, hd), BF16)

    def gating(hh):
        q = q_ref[hh]
        k = k_ref[hh]
        kmean = jnp.mean(k.astype(F32).reshape(nb, blk, hd), axis=1)
        kmean_hi = kmean.astype(BF16)
        kmean_lo = (kmean - kmean_hi.astype(F32)).astype(BF16)
        gate2 = _dot_nt(jnp.concatenate([kmean_hi, kmean_lo], axis=0), q)
        gate = gate2[:nb] + gate2[nb:]
        blk_id = lax.broadcasted_iota(jnp.int32, (nb, seq), 0)
        q_blk = lax.broadcasted_iota(jnp.int32, (nb, seq), 1) // blk
        gate = jnp.where(blk_id < q_blk, gate, -jnp.inf)
        mask_rows = []
        for n in range(nb):
            g_n = gate[n:n + 1, :]
            beats = jnp.where(gate > g_n, 1.0, jnp.where((gate == g_n) & (blk_id < n), 1.0, 0.0))
            rank = jnp.sum(beats, axis=0, keepdims=True)
            q_blk_row = q_blk[0:1, :]
            visible = ((rank < float(MOBA_TOPK)) & (n < q_blk_row)) | (n == q_blk_row)
            mask_rows.append(jnp.where(visible, 0.0, MASK_NEG))
        head = pl.program_id(1) * heads + hh
        slope_rows = [jnp.full((1, seq), slope_ref[head * ALIBI_SLOPE_PARTS + part], F32)
                      for part in range(ALIBI_SLOPE_PARTS) for _ in range(ALIBI_POS_PARTS)]
        n_ext = nb + len(slope_rows)
        mask_t = jnp.concatenate(mask_rows + slope_rows + [jnp.zeros((hd - n_ext, seq), F32)], axis=0)
        qa_ref[hh, :, :hd] = q
        qa_ref[hh, :, hd:] = mask_t.T.astype(BF16)
        ka_ref[hh, :, :hd] = k
        va_ref[hh, :, :hd] = v_ref[hh]

    causal = (lax.broadcasted_iota(jnp.int32, (blk, blk), 0)
              >= lax.broadcasted_iota(jnp.int32, (blk, blk), 1))

    def scores(hh, qb):
        return _dot_nt(qa_ref[hh, qb * blk:(qb + 1) * blk, :], ka_ref[hh, 0:(qb + 1) * blk, :])

    items = [(hh, qb) for hh in range(heads) for qb in range(nb - 1, -1, -1)]
    gating(0)
    s_next = scores(*items[0])
    for idx, (hh, qb) in enumerate(items):
        s = s_next
        if qb == nb - 1 and hh + 1 < heads:
            gating(hh + 1)
        if idx + 1 < len(items):
            s_next = scores(*items[idx + 1])
        s_own = jnp.where(causal, s[:, qb * blk:], -jnp.inf)
        s = s_own if qb == 0 else jnp.concatenate([s[:, :qb * blk], s_own], axis=1)
        m = jnp.max(s, axis=-1, keepdims=True)
        p = jnp.exp2(s - m).astype(BF16)
        acc = _dot(p, va_ref[hh, 0:(qb + 1) * blk, :])
        o_ref[qb * blk:(qb + 1) * blk, hh * hd:(hh + 1) * hd] = (acc[:, :hd] / acc[:, hd:]).astype(o_ref.dtype)


def _alibi_slope_parts():
    rest = LOG2_E * jnp.exp2(-8.0 * jnp.arange(1, N_HEADS + 1, dtype=F32) / N_HEADS)
    parts = []
    for _ in range(ALIBI_SLOPE_PARTS):
        top = lax.bitcast_convert_type(
            lax.bitcast_convert_type(rest, jnp.uint32) & jnp.uint32(0xFFFF0000), F32)
        parts.append(top)
        rest = rest - top
    return jnp.stack(parts, axis=1).reshape(-1)


def _key_augmentation(seq):
    nb = seq // MOBA_BLOCK
    t_k = jnp.arange(seq, dtype=jnp.int32)[:, None]
    lane = lax.broadcasted_iota(jnp.int32, (1, HEAD_DIM), 1)
    pos_parts = ((t_k // MOBA_BLOCK) * MOBA_BLOCK, t_k % MOBA_BLOCK)
    assert len(pos_parts) == ALIBI_POS_PARTS and MOBA_BLOCK <= 256 and nb <= 256
    table = jnp.where(lane == t_k // MOBA_BLOCK, 1, 0)
    for col in range(ALIBI_SLOPE_PARTS * ALIBI_POS_PARTS):
        table = jnp.where(lane == nb + col, pos_parts[col % ALIBI_POS_PARTS], table)
    return table.astype(BF16)


def _moba(q, k, v, *, bsz, seq):
    n_heads, t, hd = q.shape
    assert (n_heads, hd) == (N_HEADS, HEAD_DIM) == (IN_PROJ_HEADS, IN_PROJ_CHANNEL_TILE)
    hps = MOBA_HEADS_PER_STEP
    tile = pl.BlockSpec((hps, seq, hd), lambda b, h, slopes: (h, b, 0))
    return pl.pallas_call(
        _moba_kernel,
        out_shape=jax.ShapeDtypeStruct((t, n_heads * hd), BF16),
        grid_spec=pltpu.PrefetchScalarGridSpec(
            num_scalar_prefetch=1,
            grid=(bsz, N_HEADS // hps),
            in_specs=[pl.BlockSpec((seq, hd), lambda b, h, slopes: (0, 0)), tile, tile, tile],
            out_specs=pl.BlockSpec((seq, hps * hd), lambda b, h, slopes: (b, h)),
            scratch_shapes=[pltpu.VMEM((hps, seq, 2 * hd), BF16)] * 3,
        ),
        compiler_params=_params("arbitrary", "arbitrary"),
        name="moba",
    )(_alibi_slope_parts(), _key_augmentation(seq), q, k, v)


def _mix_out_kernel(a_ref, c_ref, ga_ref, gc_ref, x_ref, wa_hbm, wc_hbm, wo_hbm, o_ref,
                    w_ref, stage_ref, sems, *, chunk):
    d = o_ref.shape[1]
    n_chunks = d // chunk
    weights = (wa_hbm, wc_hbm, wo_hbm)

    @pl.when(pl.program_id(0) == 0)
    def _():
        def w_copy(idx):
            w, k = divmod(idx, n_chunks)
            return pltpu.make_async_copy(weights[w].at[pl.ds(k * chunk, chunk), :],
                                         stage_ref.at[idx % 2], sems.at[idx % 2])

        w_copy(0).start()
        for idx in range(len(weights) * n_chunks):
            if idx + 1 < len(weights) * n_chunks:
                w_copy(idx + 1).start()
            w_copy(idx).wait()
            w, k = divmod(idx, n_chunks)
            w_ref[w, pl.ds(k * chunk, chunk), :] = stage_ref[idx % 2].astype(BF16)

    a = a_ref[...]
    c = c_ref[...]
    acc = x_ref[...]
    for c0 in range(0, d, chunk):
        cols = pl.ds(c0, chunk)
        ya = _dot(a, w_ref[0, :, cols])
        yc = _dot(c, w_ref[1, :, cols])
        tiles = range(c0 // ga_ref.shape[2], (c0 + chunk) // ga_ref.shape[2])
        ga = jnp.concatenate([ga_ref[t] for t in tiles], axis=1).astype(F32)
        gc = jnp.concatenate([gc_ref[t] for t in tiles], axis=1).astype(F32)
        m = (ga * ya + gc * yc).astype(BF16)
        acc = acc + _dot(m, w_ref[2, cols, :])
    o_ref[...] = acc


def _mix_out(attn, conv, wa, wc, ga, gc, wo, x1):
    t, d = attn.shape
    tm, chunk = MIX_TOKEN_TILE, MIX_CHANNEL_CHUNK
    row = pl.BlockSpec((tm, d), lambda i: (i, 0))
    gate = pl.BlockSpec((ga.shape[0], tm, ga.shape[2]), lambda i: (0, i, 0))
    hbm = pl.BlockSpec(memory_space=pl.ANY)
    return pl.pallas_call(
        functools.partial(_mix_out_kernel, chunk=chunk),
        out_shape=jax.ShapeDtypeStruct((t, d), F32),
        grid=(t // tm,),
        in_specs=[row, row, gate, gate, row, hbm, hbm, hbm],
        out_specs=row,
        scratch_shapes=[pltpu.VMEM((3, d, d), BF16), pltpu.VMEM((2, chunk, d), F32),
                        pltpu.SemaphoreType.DMA((2,))],
        compiler_params=_params("arbitrary"),
        name="mix_out",
    )(attn, conv, ga, gc, x1, wa, wc, wo)


def kernel(x, ffn1_norm, ffn1_w_gate, ffn1_w_up, ffn1_w_down, mix_norm, w_in, b_gate, conv_w,
           w_attn_out, w_conv_out, w_out, ffn2_norm, ffn2_w_gate, ffn2_w_up, ffn2_w_down, final_norm):
    bsz, seq, d = x.shape
    depth = ffn1_norm.shape[0]
    assert d == N_HEADS * HEAD_DIM and seq % MOBA_BLOCK == 0 and seq // MOBA_BLOCK <= HEAD_DIM
    xs = x.reshape(bsz * seq, d)
    for l in range(depth):
        last = l == depth - 1
        x1, hm = _ffn(xs, ffn1_norm[l], ffn1_w_gate[l], ffn1_w_up[l], ffn1_w_down[l], mix_norm[l],
                      emit_residual=True, name="ffn1")
        q, k, v, conv, ga, gc = _in_proj(hm, w_in[l], b_gate[l], conv_w[l], bsz=bsz, seq=seq)
        attn = _moba(q, k, v, bsz=bsz, seq=seq)
        x2 = _mix_out(attn, conv, w_attn_out[l], w_conv_out[l], ga, gc, w_out[l], x1)
        if last:
            xs = _ffn(x2, ffn2_norm[l], ffn2_w_gate[l], ffn2_w_up[l], ffn2_w_down[l], final_norm,
                      emit_residual=False, name="ffn2")
        else:
            one = jnp.ones((d,), F32)
            xs, _ = _ffn(x2, ffn2_norm[l], ffn2_w_gate[l], ffn2_w_up[l], ffn2_w_down[l], one,
                         emit_residual=True, name="ffn2")
    return xs.reshape(bsz, seq, d)
```

```python
import functools

import jax
import jax.numpy as jnp
from jax import lax
from jax.experimental import pallas as pl
from jax.experimental.pallas import tpu as pltpu

F32 = jnp.float32
BF16 = jnp.bfloat16

N_HEADS = 16
HEAD_DIM = 128
CONV_KERNEL = 3
MOBA_BLOCK = 256
MOBA_TOPK = 3
MOBA_HEADS_PER_STEP = 4
ALIBI_SLOPE_PARTS = 3
ALIBI_POS_PARTS = 2
RMS_EPS = 1e-6
N_IN_SEGMENTS = 8

F32_SUBLANES = 8
V7X_VMEM_BYTES = 64 * 1024 * 1024
VMEM_LIMIT_BYTES = V7X_VMEM_BYTES - 8 * 1024 * 1024

FFN_TOKEN_TILE = 1024
FFN_DFF_TILE = 512
FFN_ROW_CHUNK = 512
FFN_X_PREFETCH_STEP = 2

IN_PROJ_CHANNEL_TILE = 128
IN_PROJ_HEADS = N_HEADS * HEAD_DIM // IN_PROJ_CHANNEL_TILE
IN_PROJ_ROW_CHUNK = 2048

MIX_TOKEN_TILE = 256
MIX_CHANNEL_CHUNK = 512

LOG2_E = 1.4426950408889634
QK_LOG2_SCALE = LOG2_E * HEAD_DIM ** -0.5

MASK_NEG = -1e30


def _params(*semantics):
    return pltpu.CompilerParams(dimension_semantics=semantics, vmem_limit_bytes=VMEM_LIMIT_BYTES)


def _rms_norm(x, g):
    return x * lax.rsqrt(jnp.mean(x * x, axis=-1, keepdims=True) + RMS_EPS) * g


def _dot(a, b):
    return jnp.dot(a, b, preferred_element_type=F32)


def _dot_nt(a, b):
    return lax.dot_general(a, b, (((1,), (1,)), ((), ())), preferred_element_type=F32)


def _ffn_kernel(x_hbm, gin_ref, wg_ref, wu_ref, wd_ref, gout_ref, *rest, emit_residual, row_chunk):
    if emit_residual:
        y_hbm, n_hbm, h_ref, acc_a, acc_b, n_buf, sems = rest
    else:
        n_hbm, h_ref, acc_a, acc_b, sems = rest
        y_hbm = n_buf = None
    tm = h_ref.shape[0]
    i = pl.program_id(0)
    f = pl.program_id(1)
    n_tiles = pl.num_programs(0)
    has_prev = i > 0
    has_next = i + 1 < n_tiles

    def hbm_tile(ref, t):
        return ref.at[pl.ds(pl.multiple_of(t * tm, tm), tm), :]

    def x_copy(t, acc):
        return pltpu.make_async_copy(hbm_tile(x_hbm, t), acc, sems.at[0])

    def acc_out_copy(t, acc):
        return pltpu.make_async_copy(acc, hbm_tile(y_hbm if emit_residual else n_hbm, t), sems.at[1])

    def n_copy(t):
        return pltpu.make_async_copy(n_buf, hbm_tile(n_hbm, t), sems.at[2])

    def step(acc_cur, acc_oth):
        @pl.when(f == 0)
        def _():
            @pl.when(i == 0)
            def _():
                x_copy(0, acc_cur).start()

            x_copy(i, acc_cur).wait()
            h_ref[...] = _rms_norm(acc_cur[...], gin_ref[...]).astype(BF16)

        @pl.when(f == FFN_X_PREFETCH_STEP)
        def _():
            @pl.when(has_prev)
            def _():
                acc_out_copy(i - 1, acc_oth).wait()

            @pl.when(has_next)
            def _():
                x_copy(i + 1, acc_oth).start()

        wg = wg_ref[...].astype(BF16)
        wu = wu_ref[...].astype(BF16)
        wd = wd_ref[...].astype(BF16)
        for r0 in range(0, tm, row_chunk):
            rows = pl.ds(r0, row_chunk)
            h = h_ref[rows, :]
            g = _dot(h, wg)
            u = _dot(h, wu)
            a = (0.5 * g * jax.nn.sigmoid(g) * u).astype(BF16)
            acc_cur[rows, :] += _dot(a, wd)

        @pl.when(f == pl.num_programs(1) - 1)
        def _():
            if emit_residual:
                @pl.when(has_prev)
                def _():
                    n_copy(i - 1).wait()

                n_buf[...] = _rms_norm(acc_cur[...], gout_ref[...]).astype(n_buf.dtype)
                n_copy(i).start()
            else:
                acc_cur[...] = _rms_norm(acc_cur[...], gout_ref[...])
            acc_out_copy(i, acc_cur).start()

            @pl.when(i == n_tiles - 1)
            def _():
                acc_out_copy(i, acc_cur).wait()
                if emit_residual:
                    n_copy(i).wait()

    parity = lax.rem(i, 2)

    @pl.when(parity == 0)
    def _():
        step(acc_a, acc_b)

    @pl.when(parity == 1)
    def _():
        step(acc_b, acc_a)


def _ffn(x, g_in, wg, wu, wd, g_out, *, emit_residual, name):
    t, d = x.shape
    d_ff = wg.shape[1]
    tm, tf = FFN_TOKEN_TILE, FFN_DFF_TILE
    assert d_ff // tf > FFN_X_PREFETCH_STEP
    hbm = pl.BlockSpec(memory_space=pl.ANY)
    vec = pl.BlockSpec((1, d), lambda i, f: (0, 0))
    n_shape = jax.ShapeDtypeStruct((t, d), BF16 if emit_residual else F32)
    return pl.pallas_call(
        functools.partial(_ffn_kernel, emit_residual=emit_residual, row_chunk=FFN_ROW_CHUNK),
        out_shape=(jax.ShapeDtypeStruct((t, d), F32), n_shape) if emit_residual else n_shape,
        grid=(t // tm, d_ff // tf),
        in_specs=[
            hbm, vec,
            pl.BlockSpec((d, tf), lambda i, f: (0, f)),
            pl.BlockSpec((d, tf), lambda i, f: (0, f)),
            pl.BlockSpec((tf, d), lambda i, f: (f, 0)),
            vec,
        ],
        out_specs=(hbm, hbm) if emit_residual else hbm,
        scratch_shapes=[pltpu.VMEM((tm, d), BF16), pltpu.VMEM((tm, d), F32), pltpu.VMEM((tm, d), F32)]
        + ([pltpu.VMEM((tm, d), BF16)] if emit_residual else []) + [pltpu.SemaphoreType.DMA((3,))],
        compiler_params=_params("arbitrary", "arbitrary"),
        name=name,
    )(x, g_in.reshape(1, d), wg, wu, wd, g_out.reshape(1, d))


def _in_proj_kernel(w_hbm, h_ref, bga_ref, bgc_ref, cw_ref, q_ref, k_ref, v_ref, cp_ref, ga_ref, gc_ref,
                    wstage_ref, wcat_ref, u_ref, sems, *, row_chunk):
    seq, tn = q_ref.shape
    d = h_ref.shape[1]
    j = pl.program_id(0)

    def w_copy(seg, tile_idx):
        col = pl.multiple_of(seg * d + tile_idx * tn, tn)
        return pltpu.make_async_copy(w_hbm.at[:, pl.ds(col, tn)], wstage_ref.at[seg], sems.at[seg])

    @pl.when(pl.program_id(1) == 0)
    def _():
        @pl.when(j == 0)
        def _():
            for seg in range(N_IN_SEGMENTS):
                w_copy(seg, 0).start()

        for seg in range(N_IN_SEGMENTS):
            w_copy(seg, j).wait()
            wcat_ref[:, seg * tn:(seg + 1) * tn] = wstage_ref[seg].astype(BF16)

        @pl.when(j + 1 < pl.num_programs(0))
        def _():
            for seg in range(N_IN_SEGMENTS):
                w_copy(seg, j + 1).start()

    halo = F32_SUBLANES
    u_ref[0:halo, :] = jnp.zeros((halo, tn), F32)
    cw = cw_ref[...]
    for r0 in range(0, seq, row_chunk):
        rows = pl.ds(r0, row_chunk)
        z = _dot(h_ref[rows, :], wcat_ref[...])
        zq, zk, zv, zb, zc, zx, zga, zgc = [z[:, seg * tn:(seg + 1) * tn] for seg in range(N_IN_SEGMENTS)]
        q_ref[rows, :] = (zq * QK_LOG2_SCALE).astype(BF16)
        k_ref[rows, :] = zk.astype(BF16)
        v_ref[rows, :] = zv.astype(BF16)
        ga_ref[rows, :] = jax.nn.sigmoid(zga + bga_ref[...]).astype(BF16)
        gc_ref[rows, :] = jax.nn.sigmoid(zgc + bgc_ref[...]).astype(BF16)
        u = zc * zx
        u_ref[pl.ds(halo + r0, row_chunk), :] = u
        conv = cw[2:3, :] * u
        for back in range(1, CONV_KERNEL):
            conv += cw[2 - back:3 - back, :] * u_ref[pl.ds(halo + r0 - back, row_chunk), :]
        cp_ref[rows, :] = (zb * conv).astype(BF16)


def _in_proj(hm, w_in, b_gate, conv_w, *, bsz, seq):
    t, d = hm.shape
    tn = IN_PROJ_CHANNEL_TILE
    nj = d // tn
    assert w_in.shape == (d, N_IN_SEGMENTS * d) and t == bsz * seq

    def b_spec(seg):
        return pl.BlockSpec((1, tn), lambda j, b, seg=seg: (0, seg * nj + j))

    tile = pl.BlockSpec((None, seq, tn), lambda j, b: (j, b, 0))
    out = jax.ShapeDtypeStruct((nj, t, tn), BF16)
    return pl.pallas_call(
        functools.partial(_in_proj_kernel, row_chunk=IN_PROJ_ROW_CHUNK),
        out_shape=(out, out, out, jax.ShapeDtypeStruct((t, d), BF16), out, out),
        grid=(nj, bsz),
        in_specs=[pl.BlockSpec(memory_space=pl.ANY), pl.BlockSpec((seq, d), lambda j, b: (b, 0)),
                  b_spec(0), b_spec(1), pl.BlockSpec((CONV_KERNEL, tn), lambda j, b: (0, j))],
        out_specs=(tile, tile, tile, pl.BlockSpec((seq, tn), lambda j, b: (b, j)), tile, tile),
        scratch_shapes=[pltpu.VMEM((N_IN_SEGMENTS, d, tn), F32),
                        pltpu.VMEM((d, N_IN_SEGMENTS * tn), BF16),
                        pltpu.VMEM((seq + F32_SUBLANES, tn), F32),
                        pltpu.SemaphoreType.DMA((N_IN_SEGMENTS,))],
        compiler_params=_params("arbitrary", "arbitrary"),
        name="in_proj",
    )(w_in, hm, b_gate.reshape(1, 2 * d), b_gate.reshape(1, 2 * d), conv_w)


def _moba_kernel(slope_ref, kaug_ref, q_ref, k_ref, v_ref, o_ref, qa_ref, ka_ref, va_ref):
    heads, seq, hd = q_ref.shape
    blk = MOBA_BLOCK
    nb = seq // blk

    @pl.when((pl.program_id(0) == 0) & (pl.program_id(1) == 0))
    def _():
        for hh in range(heads):
            ka_ref[hh, :, hd:] = kaug_ref[...]
            va_ref[hh, :, hd:] = jnp.ones((seq, hd), BF16)

    def gating(hh):
        q = q_ref[hh]
        k = k_ref[hh]
        kmean = jnp.mean(k.astype(F32).reshape(nb, blk, hd), axis=1)
        kmean_hi = kmean.astype(BF16)
        kmean_lo = (kmean - kmean_hi.astype(F32)).astype(BF16)
        gate2 = _dot_nt(jnp.concatenate([kmean_hi, kmean_lo], axis=0), q)
        gate = gate2[:nb] + gate2[nb:]
        blk_id = lax.broadcasted_iota(jnp.int32, (nb, seq), 0)
        q_blk = lax.broadcasted_iota(jnp.int32, (nb, seq), 1) // blk
        gate = jnp.where(blk_id < q_blk, gate, -jnp.inf)
        mask_rows = []
        for n in range(nb):
            g_n = gate[n:n + 1, :]
            beats = jnp.where(gate > g_n, 1.0, jnp.where((gate == g_n) & (blk_id < n), 1.0, 0.0))
            rank = jnp.sum(beats, axis=0, keepdims=True)
            q_blk_row = q_blk[0:1, :]
            visible = ((rank < float(MOBA_TOPK)) & (n < q_blk_row)) | (n == q_blk_row)
            mask_rows.append(jnp.where(visible, 0.0, MASK_NEG))
        head = pl.program_id(1) * heads + hh
        slope_rows = [jnp.full((1, seq), slope_ref[head * ALIBI_SLOPE_PARTS + part], F32)
                      for part in range(ALIBI_SLOPE_PARTS) for _ in range(ALIBI_POS_PARTS)]
        n_ext = nb + len(slope_rows)
        mask_t = jnp.concatenate(mask_rows + slope_rows + [jnp.zeros((hd - n_ext, seq), F32)], axis=0)
        qa_ref[hh, :, :hd] = q
        qa_ref[hh, :, hd:] = mask_t.T.astype(BF16)
        ka_ref[hh, :, :hd] = k
        va_ref[hh, :, :hd] = v_ref[hh]

    causal = (lax.broadcasted_iota(jnp.int32, (blk, blk), 0)
              >= lax.broadcasted_iota(jnp.int32, (blk, blk), 1))

    def scores(hh, qb):
        return _dot_nt(qa_ref[hh, qb * blk:(qb + 1) * blk, :], ka_ref[hh, 0:(qb + 1) * blk, :])

    items = [(hh, qb) for hh in range(heads) for qb in range(nb - 1, -1, -1)]
    gating(0)
    s_next = scores(*items[0])
    for idx, (hh, qb) in enumerate(items):
        s = s_next
        if qb == nb - 1 and hh + 1 < heads:
            gating(hh + 1)
        if idx + 1 < len(items):
            s_next = scores(*items[idx + 1])
        s_own = jnp.where(causal, s[:, qb * blk:], -jnp.inf)
        s = s_own if qb == 0 else jnp.concatenate([s[:, :qb * blk], s_own], axis=1)
        m = jnp.max(s, axis=-1, keepdims=True)
        p = jnp.exp2(s - m).astype(BF16)
        acc = _dot(p, va_ref[hh, 0:(qb + 1) * blk, :])
        o_ref[qb * blk:(qb + 1) * blk, hh * hd:(hh + 1) * hd] = (acc[:, :hd] / acc[:, hd:]).astype(o_ref.dtype)


def _alibi_slope_parts():
    rest = LOG2_E * jnp.exp2(-8.0 * jnp.arange(1, N_HEADS + 1, dtype=F32) / N_HEADS)
    parts = []
    for _ in range(ALIBI_SLOPE_PARTS):
        top = lax.bitcast_convert_type(
            lax.bitcast_convert_type(rest, jnp.uint32) & jnp.uint32(0xFFFF0000), F32)
        parts.append(top)
        rest = rest - top
    return jnp.stack(parts, axis=1).reshape(-1)


def _key_augmentation(seq):
    nb = seq // MOBA_BLOCK
    t_k = jnp.arange(seq, dtype=jnp.int32)[:, None]
    lane = lax.broadcasted_iota(jnp.int32, (1, HEAD_DIM), 1)
    pos_parts = ((t_k // MOBA_BLOCK) * MOBA_BLOCK, t_k % MOBA_BLOCK)
    assert len(pos_parts) == ALIBI_POS_PARTS and MOBA_BLOCK <= 256 and nb <= 256
    table = jnp.where(lane == t_k // MOBA_BLOCK, 1, 0)
    for col in range(ALIBI_SLOPE_PARTS * ALIBI_POS_PARTS):
        table = jnp.where(lane == nb + col, pos_parts[col % ALIBI_POS_PARTS], table)
    return table.astype(BF16)


def _moba(q, k, v, *, bsz, seq):
    n_heads, t, hd = q.shape
    assert (n_heads, hd) == (N_HEADS, HEAD_DIM) == (IN_PROJ_HEADS, IN_PROJ_CHANNEL_TILE)
    hps = MOBA_HEADS_PER_STEP
    tile = pl.BlockSpec((hps, seq, hd), lambda b, h, slopes: (h, b, 0))
    return pl.pallas_call(
        _moba_kernel,
        out_shape=jax.ShapeDtypeStruct((t, n_heads * hd), BF16),
        grid_spec=pltpu.PrefetchScalarGridSpec(
            num_scalar_prefetch=1,
            grid=(bsz, N_HEADS // hps),
            in_specs=[pl.BlockSpec((seq, hd), lambda b, h, slopes: (0, 0)), tile, tile, tile],
            out_specs=pl.BlockSpec((seq, hps * hd), lambda b, h, slopes: (b, h)),
            scratch_shapes=[pltpu.VMEM((hps, seq, 2 * hd), BF16)] * 3,
        ),
        compiler_params=_params("arbitrary", "arbitrary"),
        name="moba",
    )(_alibi_slope_parts(), _key_augmentation(seq), q, k, v)


def _mix_out_kernel(a_ref, c_ref, ga_ref, gc_ref, x_ref, wa_hbm, wc_hbm, wo_hbm, o_ref,
                    w_ref, stage_ref, sems, *, chunk):
    d = o_ref.shape[1]
    n_chunks = d // chunk
    weights = (wa_hbm, wc_hbm, wo_hbm)

    @pl.when(pl.program_id(0) == 0)
    def _():
        def w_copy(idx):
            w, k = divmod(idx, n_chunks)
            return pltpu.make_async_copy(weights[w].at[pl.ds(k * chunk, chunk), :],
                                         stage_ref.at[idx % 2], sems.at[idx % 2])

        w_copy(0).start()
        for idx in range(len(weights) * n_chunks):
            if idx + 1 < len(weights) * n_chunks:
                w_copy(idx + 1).start()
            w_copy(idx).wait()
            w, k = divmod(idx, n_chunks)
            w_ref[w, pl.ds(k * chunk, chunk), :] = stage_ref[idx % 2].astype(BF16)

    a = a_ref[...]
    c = c_ref[...]
    acc = x_ref[...]
    for c0 in range(0, d, chunk):
        cols = pl.ds(c0, chunk)
        ya = _dot(a, w_ref[0, :, cols])
        yc = _dot(c, w_ref[1, :, cols])
        tiles = range(c0 // ga_ref.shape[2], (c0 + chunk) // ga_ref.shape[2])
        ga = jnp.concatenate([ga_ref[t] for t in tiles], axis=1).astype(F32)
        gc = jnp.concatenate([gc_ref[t] for t in tiles], axis=1).astype(F32)
        m = (ga * ya + gc * yc).astype(BF16)
        acc = acc + _dot(m, w_ref[2, cols, :])
    o_ref[...] = acc


def _mix_out(attn, conv, wa, wc, ga, gc, wo, x1):
    t, d = attn.shape
    tm, chunk = MIX_TOKEN_TILE, MIX_CHANNEL_CHUNK
    row = pl.BlockSpec((tm, d), lambda i: (i, 0))
    gate = pl.BlockSpec((ga.shape[0], tm, ga.shape[2]), lambda i: (0, i, 0))
    hbm = pl.BlockSpec(memory_space=pl.ANY)
    return pl.pallas_call(
        functools.partial(_mix_out_kernel, chunk=chunk),
        out_shape=jax.ShapeDtypeStruct((t, d), F32),
        grid=(t // tm,),
        in_specs=[row, row, gate, gate, row, hbm, hbm, hbm],
        out_specs=row,
        scratch_shapes=[pltpu.VMEM((3, d, d), BF16), pltpu.VMEM((2, chunk, d), F32),
                        pltpu.SemaphoreType.DMA((2,))],
        compiler_params=_params("arbitrary"),
        name="mix_out",
    )(attn, conv, ga, gc, x1, wa, wc, wo)


def kernel(x, ffn1_norm, ffn1_w_gate, ffn1_w_up, ffn1_w_down, mix_norm, w_in, b_gate, conv_w,
           w_attn_out, w_conv_out, w_out, ffn2_norm, ffn2_w_gate, ffn2_w_up, ffn2_w_down, final_norm):
    bsz, seq, d = x.shape
    depth = ffn1_norm.shape[0]
    assert d == N_HEADS * HEAD_DIM and seq % MOBA_BLOCK == 0 and seq // MOBA_BLOCK <= HEAD_DIM
    xs = x.reshape(bsz * seq, d)
    for l in range(depth):
        last = l == depth - 1
        x1, hm = _ffn(xs, ffn1_norm[l], ffn1_w_gate[l], ffn1_w_up[l], ffn1_w_down[l], mix_norm[l],
                      emit_residual=True, name="ffn1")
        q, k, v, conv, ga, gc = _in_proj(hm, w_in[l], b_gate[l], conv_w[l], bsz=bsz, seq=seq)
        attn = _moba(q, k, v, bsz=bsz, seq=seq)
        x2 = _mix_out(attn, conv, w_attn_out[l], w_conv_out[l], ga, gc, w_out[l], x1)
        if last:
            xs = _ffn(x2, ffn2_norm[l], ffn2_w_gate[l], ffn2_w_up[l], ffn2_w_down[l], final_norm,
                      emit_residual=False, name="ffn2")
        else:
            one = jnp.ones((d,), F32)
            xs, _ = _ffn(x2, ffn2_norm[l], ffn2_w_gate[l], ffn2_w_up[l], ffn2_w_down[l], one,
                         emit_residual=True, name="ffn2")
    return xs.reshape(bsz, seq, d)
```

```python
import functools

import jax
import jax.numpy as jnp
from jax import lax
from jax.experimental import pallas as pl
from jax.experimental.pallas import tpu as pltpu

F32 = jnp.float32
BF16 = jnp.bfloat16

N_HEADS = 16
HEAD_DIM = 128
CONV_KERNEL = 3
MOBA_BLOCK = 256
MOBA_TOPK = 3
MOBA_HEADS_PER_STEP = 4
ALIBI_SLOPE_PARTS = 3
ALIBI_POS_PARTS = 2
RMS_EPS = 1e-6
N_IN_SEGMENTS = 8

F32_SUBLANES = 8
V7X_VMEM_BYTES = 64 * 1024 * 1024
VMEM_LIMIT_BYTES = V7X_VMEM_BYTES - 8 * 1024 * 1024

FFN_TOKEN_TILE = 1024
FFN_DFF_TILE = 512
FFN_ROW_CHUNK = 512
FFN_X_PREFETCH_STEP = 2

IN_PROJ_CHANNEL_TILE = 128
IN_PROJ_HEADS = N_HEADS * HEAD_DIM // IN_PROJ_CHANNEL_TILE
IN_PROJ_ROW_CHUNK = 1024

MIX_TOKEN_TILE = 256
MIX_CHANNEL_CHUNK = 256

LOG2_E = 1.4426950408889634
QK_LOG2_SCALE = LOG2_E * HEAD_DIM ** -0.5

MASK_NEG = -1e30


def _params(*semantics):
    return pltpu.CompilerParams(dimension_semantics=semantics, vmem_limit_bytes=VMEM_LIMIT_BYTES)


def _rms_norm(x, g):
    return x * lax.rsqrt(jnp.mean(x * x, axis=-1, keepdims=True) + RMS_EPS) * g


def _dot(a, b):
    return jnp.dot(a, b, preferred_element_type=F32)


def _dot_nt(a, b):
    return lax.dot_general(a, b, (((1,), (1,)), ((), ())), preferred_element_type=F32)


def _ffn_kernel(x_hbm, gin_ref, wg_ref, wu_ref, wd_ref, gout_ref, *rest, emit_residual, row_chunk):
    if emit_residual:
        y_hbm, n_hbm, h_ref, acc_a, acc_b, n_buf, sems = rest
    else:
        n_hbm, h_ref, acc_a, acc_b, sems = rest
        y_hbm = n_buf = None
    tm = h_ref.shape[0]
    i = pl.program_id(0)
    f = pl.program_id(1)
    n_tiles = pl.num_programs(0)
    has_prev = i > 0
    has_next = i + 1 < n_tiles

    def hbm_tile(ref, t):
        return ref.at[pl.ds(pl.multiple_of(t * tm, tm), tm), :]

    def x_copy(t, acc):
        return pltpu.make_async_copy(hbm_tile(x_hbm, t), acc, sems.at[0])

    def acc_out_copy(t, acc):
        return pltpu.make_async_copy(acc, hbm_tile(y_hbm if emit_residual else n_hbm, t), sems.at[1])

    def n_copy(t):
        return pltpu.make_async_copy(n_buf, hbm_tile(n_hbm, t), sems.at[2])

    def step(acc_cur, acc_oth):
        @pl.when(f == 0)
        def _():
            @pl.when(i == 0)
            def _():
                x_copy(0, acc_cur).start()

            x_copy(i, acc_cur).wait()
            h_ref[...] = _rms_norm(acc_cur[...], gin_ref[...]).astype(BF16)

        @pl.when(f == FFN_X_PREFETCH_STEP)
        def _():
            @pl.when(has_prev)
            def _():
                acc_out_copy(i - 1, acc_oth).wait()

            @pl.when(has_next)
            def _():
                x_copy(i + 1, acc_oth).start()

        wg = wg_ref[...].astype(BF16)
        wu = wu_ref[...].astype(BF16)
        wd = wd_ref[...].astype(BF16)
        for r0 in range(0, tm, row_chunk):
            rows = pl.ds(r0, row_chunk)
            h = h_ref[rows, :]
            g = _dot(h, wg)
            u = _dot(h, wu)
            a = (0.5 * g * jax.nn.sigmoid(g) * u).astype(BF16)
            acc_cur[rows, :] += _dot(a, wd)

        @pl.when(f == pl.num_programs(1) - 1)
        def _():
            if emit_residual:
                @pl.when(has_prev)
                def _():
                    n_copy(i - 1).wait()

                n_buf[...] = _rms_norm(acc_cur[...], gout_ref[...]).astype(n_buf.dtype)
                n_copy(i).start()
            else:
                acc_cur[...] = _rms_norm(acc_cur[...], gout_ref[...])
            acc_out_copy(i, acc_cur).start()

            @pl.when(i == n_tiles - 1)
            def _():
                acc_out_copy(i, acc_cur).wait()
                if emit_residual:
                    n_copy(i).wait()

    parity = lax.rem(i, 2)

    @pl.when(parity == 0)
    def _():
        step(acc_a, acc_b)

    @pl.when(parity == 1)
    def _():
        step(acc_b, acc_a)


def _ffn(x, g_in, wg, wu, wd, g_out, *, emit_residual, name):
    t, d = x.shape
    d_ff = wg.shape[1]
    tm, tf = FFN_TOKEN_TILE, FFN_DFF_TILE
    assert d_ff // tf > FFN_X_PREFETCH_STEP
    hbm = pl.BlockSpec(memory_space=pl.ANY)
    vec = pl.BlockSpec((1, d), lambda i, f: (0, 0))
    n_shape = jax.ShapeDtypeStruct((t, d), BF16 if emit_residual else F32)
    return pl.pallas_call(
        functools.partial(_ffn_kernel, emit_residual=emit_residual, row_chunk=FFN_ROW_CHUNK),
        out_shape=(jax.ShapeDtypeStruct((t, d), F32), n_shape) if emit_residual else n_shape,
        grid=(t // tm, d_ff // tf),
        in_specs=[
            hbm, vec,
            pl.BlockSpec((d, tf), lambda i, f: (0, f)),
            pl.BlockSpec((d, tf), lambda i, f: (0, f)),
            pl.BlockSpec((tf, d), lambda i, f: (f, 0)),
            vec,
        ],
        out_specs=(hbm, hbm) if emit_residual else hbm,
        scratch_shapes=[pltpu.VMEM((tm, d), BF16), pltpu.VMEM((tm, d), F32), pltpu.VMEM((tm, d), F32)]
        + ([pltpu.VMEM((tm, d), BF16)] if emit_residual else []) + [pltpu.SemaphoreType.DMA((3,))],
        compiler_params=_params("arbitrary", "arbitrary"),
        name=name,
    )(x, g_in.reshape(1, d), wg, wu, wd, g_out.reshape(1, d))


def _in_proj_kernel(w_hbm, h_ref, bga_ref, bgc_ref, cw_ref, q_ref, k_ref, v_ref, cp_ref, ga_ref, gc_ref,
                    wstage_ref, wcat_ref, u_ref, sems, *, row_chunk):
    seq, tn = q_ref.shape
    d = h_ref.shape[1]
    j = pl.program_id(0)

    def w_copy(seg, tile_idx):
        col = pl.multiple_of(seg * d + tile_idx * tn, tn)
        return pltpu.make_async_copy(w_hbm.at[:, pl.ds(col, tn)], wstage_ref.at[seg], sems.at[seg])

    @pl.when(pl.program_id(1) == 0)
    def _():
        @pl.when(j == 0)
        def _():
            for seg in range(N_IN_SEGMENTS):
                w_copy(seg, 0).start()

        for seg in range(N_IN_SEGMENTS):
            w_copy(seg, j).wait()
            wcat_ref[:, seg * tn:(seg + 1) * tn] = wstage_ref[seg].astype(BF16)

        @pl.when(j + 1 < pl.num_programs(0))
        def _():
            for seg in range(N_IN_SEGMENTS):
                w_copy(seg, j + 1).start()

    halo = F32_SUBLANES
    u_ref[0:halo, :] = jnp.zeros((halo, tn), F32)
    cw = cw_ref[...]
    for r0 in range(0, seq, row_chunk):
        rows = pl.ds(r0, row_chunk)
        z = _dot(h_ref[rows, :], wcat_ref[...])
        zq, zk, zv, zb, zc, zx, zga, zgc = [z[:, seg * tn:(seg + 1) * tn] for seg in range(N_IN_SEGMENTS)]
        q_ref[rows, :] = (zq * QK_LOG2_SCALE).astype(BF16)
        k_ref[rows, :] = zk.astype(BF16)
        v_ref[rows, :] = zv.astype(BF16)
        ga_ref[rows, :] = jax.nn.sigmoid(zga + bga_ref[...]).astype(BF16)
        gc_ref[rows, :] = jax.nn.sigmoid(zgc + bgc_ref[...]).astype(BF16)
        u = zc * zx
        u_ref[pl.ds(halo + r0, row_chunk), :] = u
        conv = cw[2:3, :] * u
        for back in range(1, CONV_KERNEL):
            conv += cw[2 - back:3 - back, :] * u_ref[pl.ds(halo + r0 - back, row_chunk), :]
        cp_ref[rows, :] = (zb * conv).astype(BF16)


def _in_proj(hm, w_in, b_gate, conv_w, *, bsz, seq):
    t, d = hm.shape
    tn = IN_PROJ_CHANNEL_TILE
    nj = d // tn
    assert w_in.shape == (d, N_IN_SEGMENTS * d) and t == bsz * seq

    def b_spec(seg):
        return pl.BlockSpec((1, tn), lambda j, b, seg=seg: (0, seg * nj + j))

    tile = pl.BlockSpec((None, seq, tn), lambda j, b: (j, b, 0))
    out = jax.ShapeDtypeStruct((nj, t, tn), BF16)
    return pl.pallas_call(
        functools.partial(_in_proj_kernel, row_chunk=IN_PROJ_ROW_CHUNK),
        out_shape=(out, out, out, jax.ShapeDtypeStruct((t, d), BF16), out, out),
        grid=(nj, bsz),
        in_specs=[pl.BlockSpec(memory_space=pl.ANY), pl.BlockSpec((seq, d), lambda j, b: (b, 0)),
                  b_spec(0), b_spec(1), pl.BlockSpec((CONV_KERNEL, tn), lambda j, b: (0, j))],
        out_specs=(tile, tile, tile, pl.BlockSpec((seq, tn), lambda j, b: (b, j)), tile, tile),
        scratch_shapes=[pltpu.VMEM((N_IN_SEGMENTS, d, tn), F32),
                        pltpu.VMEM((d, N_IN_SEGMENTS * tn), BF16),
                        pltpu.VMEM((seq + F32_SUBLANES, tn), F32),
                        pltpu.SemaphoreType.DMA((N_IN_SEGMENTS,))],
        compiler_params=_params("arbitrary", "arbitrary"),
        name="in_proj",
    )(w_in, hm, b_gate.reshape(1, 2 * d), b_gate.reshape(1, 2 * d), conv_w)


def _moba_kernel(slope_ref, kaug_ref, q_ref, k_ref, v_ref, o_ref, qa_ref, ka_ref, va_ref):
    heads, seq, hd = q_ref.shape
    blk = MOBA_BLOCK
    nb = seq // blk

    @pl.when((pl.program_id(0) == 0) & (pl.program_id(1) == 0))
    def _():
        for hh in range(heads):
            ka_ref[hh, :, hd:] = kaug_ref[...]
            va_ref[hh, :, hd:] = jnp.ones((seq, hd), BF16)

    def gating(hh):
        q = q_ref[hh]
        k = k_ref[hh]
        kmean = jnp.mean(k.astype(F32).reshape(nb, blk, hd), axis=1)
        kmean_hi = kmean.astype(BF16)
        kmean_lo = (kmean - kmean_hi.astype(F32)).astype(BF16)
        gate2 = _dot_nt(jnp.concatenate([kmean_hi, kmean_lo], axis=0), q)
        gate = gate2[:nb] + gate2[nb:]
        blk_id = lax.broadcasted_iota(jnp.int32, (nb, seq), 0)
        q_blk = lax.broadcasted_iota(jnp.int32, (nb, seq), 1) // blk
        gate = jnp.where(blk_id < q_blk, gate, -jnp.inf)
        mask_rows = []
        for n in range(nb):
            g_n = gate[n:n + 1, :]
            beats = jnp.where(gate > g_n, 1.0, jnp.where((gate == g_n) & (blk_id < n), 1.0, 0.0))
            rank = jnp.sum(beats, axis=0, keepdims=True)
            q_blk_row = q_blk[0:1, :]
            visible = ((rank < float(MOBA_TOPK)) & (n < q_blk_row)) | (n == q_blk_row)
            mask_rows.append(jnp.where(visible, 0.0, MASK_NEG))
        head = pl.program_id(1) * heads + hh
        slope_rows = [jnp.full((1, seq), slope_ref[head * ALIBI_SLOPE_PARTS + part], F32)
                      for part in range(ALIBI_SLOPE_PARTS) for _ in range(ALIBI_POS_PARTS)]
        n_ext = nb + len(slope_rows)
        mask_t = jnp.concatenate(mask_rows + slope_rows + [jnp.zeros((hd - n_ext, seq), F32)], axis=0)
        qa_ref[hh, :, :hd] = q
        qa_ref[hh, :, hd:] = mask_t.T.astype(BF16)
        ka_ref[hh, :, :hd] = k
        va_ref[hh, :, :hd] = v_ref[hh]

    causal = (lax.broadcasted_iota(jnp.int32, (blk, blk), 0)
              >= lax.broadcasted_iota(jnp.int32, (blk, blk), 1))

    def scores(hh, qb):
        return _dot_nt(qa_ref[hh, qb * blk:(qb + 1) * blk, :], ka_ref[hh, 0:(qb + 1) * blk, :])

    items = [(hh, qb) for hh in range(heads) for qb in range(nb - 1, -1, -1)]
    gating(0)
    s_next = scores(*items[0])
    for idx, (hh, qb) in enumerate(items):
        s = s_next
        if qb == nb - 1 and hh + 1 < heads:
            gating(hh + 1)
        if idx + 1 < len(items):
            s_next = scores(*items[idx + 1])
        s_own = jnp.where(causal, s[:, qb * blk:], -jnp.inf)
        s = s_own if qb == 0 else jnp.concatenate([s[:, :qb * blk], s_own], axis=1)
        m = jnp.max(s, axis=-1, keepdims=True)
        p = jnp.exp2(s - m).astype(BF16)
        acc = _dot(p, va_ref[hh, 0:(qb + 1) * blk, :])
        o_ref[qb * blk:(qb + 1) * blk, hh * hd:(hh + 1) * hd] = (acc[:, :hd] / acc[:, hd:]).astype(o_ref.dtype)


def _alibi_slope_parts():
    rest = LOG2_E * jnp.exp2(-8.0 * jnp.arange(1, N_HEADS + 1, dtype=F32) / N_HEADS)
    parts = []
    for _ in range(ALIBI_SLOPE_PARTS):
        top = lax.bitcast_convert_type(
            lax.bitcast_convert_type(rest, jnp.uint32) & jnp.uint32(0xFFFF0000), F32)
        parts.append(top)
        rest = rest - top
    return jnp.stack(parts, axis=1).reshape(-1)


def _key_augmentation(seq):
    nb = seq // MOBA_BLOCK
    t_k = jnp.arange(seq, dtype=jnp.int32)[:, None]
    lane = lax.broadcasted_iota(jnp.int32, (1, HEAD_DIM), 1)
    pos_parts = ((t_k // MOBA_BLOCK) * MOBA_BLOCK, t_k % MOBA_BLOCK)
    assert len(pos_parts) == ALIBI_POS_PARTS and MOBA_BLOCK <= 256 and nb <= 256
    table = jnp.where(lane == t_k // MOBA_BLOCK, 1, 0)
    for col in range(ALIBI_SLOPE_PARTS * ALIBI_POS_PARTS):
        table = jnp.where(lane == nb + col, pos_parts[col % ALIBI_POS_PARTS], table)
    return table.astype(BF16)


def _moba(q, k, v, *, bsz, seq):
    n_heads, t, hd = q.shape
    assert (n_heads, hd) == (N_HEADS, HEAD_DIM) == (IN_PROJ_HEADS, IN_PROJ_CHANNEL_TILE)
    hps = MOBA_HEADS_PER_STEP
    tile = pl.BlockSpec((hps, seq, hd), lambda b, h, slopes: (h, b, 0))
    return pl.pallas_call(
        _moba_kernel,
        out_shape=jax.ShapeDtypeStruct((t, n_heads * hd), BF16),
        grid_spec=pltpu.PrefetchScalarGridSpec(
            num_scalar_prefetch=1,
            grid=(bsz, N_HEADS // hps),
            in_specs=[pl.BlockSpec((seq, hd), lambda b, h, slopes: (0, 0)), tile, tile, tile],
            out_specs=pl.BlockSpec((seq, hps * hd), lambda b, h, slopes: (b, h)),
            scratch_shapes=[pltpu.VMEM((hps, seq, 2 * hd), BF16)] * 3,
        ),
        compiler_params=_params("arbitrary", "arbitrary"),
        name="moba",
    )(_alibi_slope_parts(), _key_augmentation(seq), q, k, v)


def _mix_out_kernel(a_ref, c_ref, ga_ref, gc_ref, x_ref, wa_hbm, wc_hbm, wo_hbm, o_ref,
                    w_ref, stage_ref, sems, *, chunk):
    d = o_ref.shape[1]
    n_chunks = d // chunk
    weights = (wa_hbm, wc_hbm, wo_hbm)

    @pl.when(pl.program_id(0) == 0)
    def _():
        def w_copy(idx):
            w, k = divmod(idx, n_chunks)
            return pltpu.make_async_copy(weights[w].at[pl.ds(k * chunk, chunk), :],
                                         stage_ref.at[idx % 2], sems.at[idx % 2])

        w_copy(0).start()
        for idx in range(len(weights) * n_chunks):
            if idx + 1 < len(weights) * n_chunks:
                w_copy(idx + 1).start()
            w_copy(idx).wait()
            w, k = divmod(idx, n_chunks)
            w_ref[w, pl.ds(k * chunk, chunk), :] = stage_ref[idx % 2].astype(BF16)

    a = a_ref[...]
    c = c_ref[...]
    acc = x_ref[...]
    for c0 in range(0, d, chunk):
        cols = pl.ds(c0, chunk)
        ya = _dot(a, w_ref[0, :, cols])
        yc = _dot(c, w_ref[1, :, cols])
        tiles = range(c0 // ga_ref.shape[2], (c0 + chunk) // ga_ref.shape[2])
        ga = jnp.concatenate([ga_ref[t] for t in tiles], axis=1).astype(F32)
        gc = jnp.concatenate([gc_ref[t] for t in tiles], axis=1).astype(F32)
        m = (ga * ya + gc * yc).astype(BF16)
        acc = acc + _dot(m, w_ref[2, cols, :])
    o_ref[...] = acc


def _mix_out(attn, conv, wa, wc, ga, gc, wo, x1):
    t, d = attn.shape
    tm, chunk = MIX_TOKEN_TILE, MIX_CHANNEL_CHUNK
    row = pl.BlockSpec((tm, d), lambda i: (i, 0))
    gate = pl.BlockSpec((ga.shape[0], tm, ga.shape[2]), lambda i: (0, i, 0))
    hbm = pl.BlockSpec(memory_space=pl.ANY)
    return pl.pallas_call(
        functools.partial(_mix_out_kernel, chunk=chunk),
        out_shape=jax.ShapeDtypeStruct((t, d), F32),
        grid=(t // tm,),
        in_specs=[row, row, gate, gate, row, hbm, hbm, hbm],
        out_specs=row,
        scratch_shapes=[pltpu.VMEM((3, d, d), BF16), pltpu.VMEM((2, chunk, d), F32),
                        pltpu.SemaphoreType.DMA((2,))],
        compiler_params=_params("arbitrary"),
        name="mix_out",
    )(attn, conv, ga, gc, x1, wa, wc, wo)


def kernel(x, ffn1_norm, ffn1_w_gate, ffn1_w_up, ffn1_w_down, mix_norm, w_in, b_gate, conv_w,
           w_attn_out, w_conv_out, w_out, ffn2_norm, ffn2_w_gate, ffn2_w_up, ffn2_w_down, final_norm):
    bsz, seq, d = x.shape
    depth = ffn1_norm.shape[0]
    assert d == N_HEADS * HEAD_DIM and seq % MOBA_BLOCK == 0 and seq // MOBA_BLOCK <= HEAD_DIM
    xs = x.reshape(bsz * seq, d)
    for l in range(depth):
        last = l == depth - 1
        x1, hm = _ffn(xs, ffn1_norm[l], ffn1_w_gate[l], ffn1_w_up[l], ffn1_w_down[l], mix_norm[l],
                      emit_residual=True, name="ffn1")
        q, k, v, conv, ga, gc = _in_proj(hm, w_in[l], b_gate[l], conv_w[l], bsz=bsz, seq=seq)
        attn = _moba(q, k, v, bsz=bsz, seq=seq)
        x2 = _mix_out(attn, conv, w_attn_out[l], w_conv_out[l], ga, gc, w_out[l], x1)
        if last:
            xs = _ffn(x2, ffn2_norm[l], ffn2_w_gate[l], ffn2_w_up[l], ffn2_w_down[l], final_norm,
                      emit_residual=False, name="ffn2")
        else:
            one = jnp.ones((d,), F32)
            xs, _ = _ffn(x2, ffn2_norm[l], ffn2_w_gate[l], ffn2_w_up[l], ffn2_w_down[l], one,
                         emit_residual=True, name="ffn2")
    return xs.reshape(bsz, seq, d)
```
